```python
import math
import jax, jax.numpy as jnp
from jax import lax
import numpy as np

D_MODEL = 1024
BATCH = 4
SEQ = 4096
DEPTH = 2
DEC_BATCH = 32
DEC_SEQ = 4
PAST_LEN = 8192
PAGE_SIZE = 128

M_HEADS = 4
M_HEAD_DIM = D_MODEL // 8
M_WIDTH = M_HEADS * M_HEAD_DIM
CONV_WIDTH = 4
M_CHUNK = 64
A_HEAD_DIM = 64
A_HEADS = (D_MODEL // 2) // A_HEAD_DIM
A_KV_HEADS = A_HEADS // 2
A_GROUP = A_HEADS // A_KV_HEADS
A_WIDTH = A_HEADS * A_HEAD_DIM
A_KV_WIDTH = A_KV_HEADS * A_HEAD_DIM
IDX_HEADS = 8
IDX_DIM = 64
TOPK_MAX = 256
Q_BLOCK = 128
ROPE_THETA = 10000.0
D_FF = ((8 * D_MODEL // 3 + 255) // 256) * 256
NORM_EPS = 1e-6
PROJ_SPLITS = (M_WIDTH, M_WIDTH, M_WIDTH, M_WIDTH, M_HEADS, M_HEADS,
               A_WIDTH, A_KV_WIDTH, A_KV_WIDTH, IDX_HEADS * IDX_DIM, IDX_DIM, IDX_HEADS)
D_IN = sum(PROJ_SPLITS)
MIX_WIDTH = M_WIDTH + A_WIDTH

kernel_name = "hybrid_mlstm_dsa_decoder_step"


def rmsnorm(x, g):
    xf = x.astype(jnp.float32)
    y = xf * lax.rsqrt(jnp.mean(xf * xf, axis=-1, keepdims=True) + NORM_EPS)
    return (y * g.astype(jnp.float32)).astype(x.dtype)


def rope(x, pos):
    half = x.shape[-1] // 2
    inv = ROPE_THETA ** (-jnp.arange(half, dtype=jnp.float32) / half)
    ang = pos.astype(jnp.float32)[:, None] * inv[None, :]
    cos = jnp.cos(ang)[None, :, None, :]
    sin = jnp.sin(ang)[None, :, None, :]
    xf = x.astype(jnp.float32)
    x1, x2 = xf[..., :half], xf[..., half:]
    return jnp.concatenate([x1 * cos - x2 * sin, x2 * cos + x1 * sin], axis=-1).astype(x.dtype)


def gather_rows(rows, idx):
    return jax.vmap(lambda r, i: r[i])(rows, idx)


def project(x, g_mix_l, w_in_l):
    p = rmsnorm(x, g_mix_l) @ w_in_l
    parts, start = [], 0
    for width in PROJ_SPLITS:
        parts.append(p[..., start:start + width])
        start += width
    return parts


def causal_conv(u, buf, w):
    T = u.shape[1]
    full = jnp.concatenate([buf, u], axis=1)
    out = full[:, 0:T] * w[0]
    for j in range(1, CONV_WIDTH):
        out = out + full[:, j:j + T] * w[j]
    return out, full[:, -(CONV_WIDTH - 1):]


def mlstm_chunkwise(q, k, v, ig, lf, C0, n0, m0):
    B, T, H, d = q.shape
    L = math.gcd(T, M_CHUNK)
    NC = T // L

    def to_chunks(a):
        a = a.reshape((B, NC, L, H) + a.shape[3:])
        return jnp.moveaxis(a, (1, 3), (0, 2))

    f32 = jnp.float32
    xs = (to_chunks(q.astype(f32)), to_chunks(k.astype(f32)), to_chunks(v.astype(f32)),
          to_chunks(ig.astype(f32)), to_chunks(lf.astype(f32)))
    causal = jnp.tril(jnp.ones((L, L), dtype=bool))

    def step(carry, xc):
        C, n, m = carry
        qb, kb, vb, ib, fb = xc
        b = jnp.cumsum(fb, axis=-1)
        logD = b[..., :, None] - b[..., None, :] + ib[..., None, :]
        logD = jnp.where(causal, logD, -jnp.inf)
        inter = b + m[..., None]
        m_t = jnp.maximum(jnp.max(logD, axis=-1), inter)
        w_inter = jnp.exp(inter - m_t)
        s = jnp.einsum('bhtd,bhsd->bhts', qb, kb) * jnp.exp(logD - m_t[..., None])
        num = (w_inter[..., None] * jnp.einsum('bhvd,bhtd->bhtv', C, qb)
               + jnp.einsum('bhts,bhsv->bhtv', s, vb))
        nq = w_inter * jnp.einsum('bhd,bhtd->bht', n, qb) + jnp.sum(s, axis=-1)
        h = num / jnp.maximum(jnp.abs(nq), jnp.exp(-m_t))[..., None]
        bL = b[..., -1]
        log_w = bL[..., None] - b + ib
        m_new = jnp.maximum(bL + m, jnp.max(log_w, axis=-1))
        decay = jnp.exp(bL + m - m_new)
        wk = jnp.exp(log_w - m_new[..., None])
        C_new = decay[..., None, None] * C + jnp.einsum('bhs,bhsv,bhsd->bhvd', wk, vb, kb)
        n_new = decay[..., None] * n + jnp.einsum('bhs,bhsd->bhd', wk, kb)
        return (C_new, n_new, m_new), h

    (C, n, m), hs = lax.scan(step, (C0.astype(f32), n0.astype(f32), m0.astype(f32)), xs)
    h = jnp.moveaxis(hs, (0, 2), (1, 3)).reshape(B, T, H, d)
    return h, C, n, m


def mlstm_mixer(mq, mk, mv, mo, mi, mf, conv_buf, C0, n0, m0, conv_w_l, b_gate_l, g_mout_l):
    B, T, _ = mq.shape
    qk, new_buf = causal_conv(jnp.concatenate([mq, mk], axis=-1), conv_buf.astype(mq.dtype), conv_w_l)
    qk = jax.nn.silu(qk)
    shp = (B, T, M_HEADS, M_HEAD_DIM)
    q = qk[..., :M_WIDTH].reshape(shp)
    k = qk[..., M_WIDTH:].reshape(shp) * (M_HEAD_DIM ** -0.5)
    v = mv.reshape(shp)
    bg = b_gate_l.astype(jnp.float32)
    ig = mi.astype(jnp.float32) + bg[:M_HEADS]
    lf = jax.nn.log_sigmoid(mf.astype(jnp.float32) + bg[M_HEADS:])
    h, C, n, m = mlstm_chunkwise(q, k, v, ig, lf, C0, n0, m0)
    h = rmsnorm(h, g_mout_l.reshape(M_HEADS, M_HEAD_DIM)).reshape(B, T, M_WIDTH)
    return jax.nn.sigmoid(mo.astype(jnp.float32)) * h, (C, n, m, new_buf)


def attn_inputs(aq, ak, av, iq, ik, pos, g_q_l, g_k_l):
    B, T, _ = aq.shape
    q = rope(rmsnorm(aq.reshape(B, T, A_HEADS, A_HEAD_DIM), g_q_l), pos)
    k = rope(rmsnorm(ak.reshape(B, T, A_KV_HEADS, A_HEAD_DIM), g_k_l), pos)
    v = av.reshape(B, T, A_KV_HEADS, A_HEAD_DIM)
    qi = rope(iq.reshape(B, T, IDX_HEADS, IDX_DIM), pos)
    ki = rope(ik.reshape(B, T, 1, IDX_DIM), pos)[:, :, 0]
    return q, k, v, qi, ki


def indexer_scores(qi, w, ki):
    dots = jnp.einsum('bthd,bsd->bths', qi.astype(jnp.float32), ki.astype(jnp.float32)) * (IDX_DIM ** -0.5)
    return jnp.einsum('bth,bths->bts', w.astype(jnp.float32) * (IDX_HEADS ** -0.5), jax.nn.relu(dots))


def select_keys(qi, w, ki, qpos, key_pos, topk):
    scores = indexer_scores(qi, w, ki)
    visible = key_pos[None, :] <= qpos[:, None]
    scores = jnp.where(visible[None], scores, -jnp.inf)
    _, idx = lax.top_k(scores, topk)
    valid = idx <= qpos[None, :, None]
    return idx, valid


def attend_selected(q, kg, vg, valid):
    B, T = q.shape[:2]
    qg = q.reshape(B, T, A_KV_HEADS, A_GROUP, A_HEAD_DIM).astype(jnp.float32)
    s = jnp.einsum('btkgd,btnkd->btkgn', qg, kg.astype(jnp.float32)) * (A_HEAD_DIM ** -0.5)
    s = jnp.where(valid[:, :, None, None, :], s, -jnp.inf)
    p = jax.nn.softmax(s, axis=-1)
    o = jnp.einsum('btkgn,btnkd->btkgd', p, vg.astype(jnp.float32))
    return o.reshape(B, T, A_WIDTH)


def dsa_prompt(q, k, v, qi, ki, w):
    B, S = q.shape[:2]
    topk = min(TOPK_MAX, S // 4)
    nb = S // Q_BLOCK
    key_pos = jnp.arange(S, dtype=jnp.int32)

    def blocks(a):
        return jnp.moveaxis(a.reshape((B, nb, Q_BLOCK) + a.shape[2:]), 1, 0)

    def one_block(args):
        qb, qib, wb, start = args
        qpos = start + jnp.arange(Q_BLOCK, dtype=jnp.int32)
        idx, valid = select_keys(qib, wb, ki, qpos, key_pos, topk)
        return attend_selected(qb, gather_rows(k, idx), gather_rows(v, idx), valid)

    starts = jnp.arange(nb, dtype=jnp.int32) * Q_BLOCK
    out = lax.map(one_block, (blocks(q), blocks(qi), blocks(w), starts))
    return jnp.moveaxis(out, 0, 1).reshape(B, S, A_WIDTH)


def dsa_sample(q, k, v, qi, ki, w, pos, l, cache_k, cache_v, cache_kidx, page_table):
    B, T = q.shape[:2]
    past = page_table.shape[1] * PAGE_SIZE
    L = past + T
    topk = min(TOPK_MAX, L // 4)
    ki_past = cache_kidx[l, page_table].reshape(B, past, IDX_DIM)
    ki_all = jnp.concatenate([ki_past.astype(ki.dtype), ki], axis=1)
    idx, valid = select_keys(qi, w, ki_all, pos, jnp.arange(L, dtype=jnp.int32), topk)
    in_past = (idx < past)[..., None, None]
    pidx = jnp.minimum(idx, past - 1)
    phys = gather_rows(page_table, pidx // PAGE_SIZE)
    off = pidx % PAGE_SIZE
    nidx = jnp.clip(idx - past, 0, T - 1)
    kg = jnp.where(in_past, cache_k[l, phys, off].astype(k.dtype), gather_rows(k, nidx))
    vg = jnp.where(in_past, cache_v[l, phys, off].astype(v.dtype), gather_rows(v, nidx))
    return attend_selected(q, kg, vg, valid)


def finish_layer(x, m_out, a_out, w_out_l, g_ffn_l, w_gu_l, w_down_l):
    mix = jnp.concatenate([m_out.astype(x.dtype), a_out.astype(x.dtype)], axis=-1)
    x = x + mix @ w_out_l
    g, u = jnp.split(rmsnorm(x, g_ffn_l) @ w_gu_l, 2, axis=-1)
    return x + (jax.nn.silu(g) * u) @ w_down_l


def setup_inputs(seed: int = 0) -> dict:
    key = jax.random.key(seed)
    ks = jax.random.split(key, 24)
    n_pages = PAST_LEN // PAGE_SIZE
    n_used = DEC_BATCH * n_pages
    n_pool = n_used + n_used // 4

    def nrm(k, shape, scale=1.0):
        return scale * jax.random.normal(k, shape, jnp.float32)

    page_table = jax.random.permutation(ks[5], n_pool)[:n_used].reshape(DEC_BATCH, n_pages).astype(jnp.int32)
    b_gate = jnp.concatenate([nrm(ks[13], (DEPTH, M_HEADS), 0.1),
                              3.0 + nrm(ks[14], (DEPTH, M_HEADS), 0.5)], axis=-1)
    return {
        "x_prompt": nrm(ks[0], (BATCH, SEQ, D_MODEL)),
        "x_sample": nrm(ks[1], (DEC_BATCH, DEC_SEQ, D_MODEL)),
        "cache_k": nrm(ks[2], (DEPTH, n_pool, PAGE_SIZE, A_KV_HEADS, A_HEAD_DIM)),
        "cache_v": nrm(ks[3], (DEPTH, n_pool, PAGE_SIZE, A_KV_HEADS, A_HEAD_DIM)),
        "cache_kidx": nrm(ks[4], (DEPTH, n_pool, PAGE_SIZE, IDX_DIM)),
        "page_table": page_table,
        "state_C": nrm(ks[6], (DEPTH, DEC_BATCH, M_HEADS, M_HEAD_DIM, M_HEAD_DIM), 0.1),
        "state_n": nrm(ks[7], (DEPTH, DEC_BATCH, M_HEADS, M_HEAD_DIM), 0.1),
        "state_m": nrm(ks[8], (DEPTH, DEC_BATCH, M_HEADS)),
        "state_conv": nrm(ks[9], (DEPTH, DEC_BATCH, CONV_WIDTH - 1, 2 * M_WIDTH)),
        "g_mix": 1.0 + nrm(ks[10], (DEPTH, D_MODEL), 0.1),
        "w_in": nrm(ks[11], (DEPTH, D_MODEL, D_IN), D_MODEL ** -0.5),
        "conv_w": nrm(ks[12], (DEPTH, CONV_WIDTH, 2 * M_WIDTH), CONV_WIDTH ** -0.5),
        "b_gate": b_gate,
        "g_mout": 1.0 + nrm(ks[15], (DEPTH, M_WIDTH), 0.1),
        "g_q": 1.0 + nrm(ks[16], (DEPTH, A_HEAD_DIM), 0.1),
        "g_k": 1.0 + nrm(ks[17], (DEPTH, A_HEAD_DIM), 0.1),
        "w_out": nrm(ks[18], (DEPTH, MIX_WIDTH, D_MODEL), MIX_WIDTH ** -0.5),
        "g_ffn": 1.0 + nrm(ks[19], (DEPTH, D_MODEL), 0.1),
        "w_gate_up": nrm(ks[20], (DEPTH, D_MODEL, 2 * D_FF), D_MODEL ** -0.5),
        "w_down": nrm(ks[21], (DEPTH, D_FF, D_MODEL), D_FF ** -0.5),
    }


def reference(x_prompt, x_sample, cache_k, cache_v, cache_kidx, page_table, state_C, state_n, state_m,
              state_conv, g_mix, w_in, conv_w, b_gate, g_mout, g_q, g_k, w_out, g_ffn, w_gate_up, w_down):
    B, S, _ = x_prompt.shape
    DB, T, _ = x_sample.shape
    past = page_table.shape[1] * PAGE_SIZE
    pos_p = jnp.arange(S, dtype=jnp.int32)
    pos_s = past + jnp.arange(T, dtype=jnp.int32)
    xp, xs = x_prompt, x_sample
    kp, vp, kip, Cp, Np, Mp, Bp = [], [], [], [], [], [], []
    ks_, vs_, kis, Cs, Ns, Ms, Bs = [], [], [], [], [], [], []
    for l in range(DEPTH):
        mq, mk, mv, mo, mi, mf, aq, ak, av, iq, ik, iw = project(xp, g_mix[l], w_in[l])
        m_out, (C, n, m, buf) = mlstm_mixer(
            mq, mk, mv, mo, mi, mf, jnp.zeros((B, CONV_WIDTH - 1, 2 * M_WIDTH), xp.dtype),
            jnp.zeros((B, M_HEADS, M_HEAD_DIM, M_HEAD_DIM), jnp.float32),
            jnp.zeros((B, M_HEADS, M_HEAD_DIM), jnp.float32), jnp.zeros((B, M_HEADS), jnp.float32),
            conv_w[l], b_gate[l], g_mout[l])
        q, k, v, qi, ki = attn_inputs(aq, ak, av, iq, ik, pos_p, g_q[l], g_k[l])
        a_out = dsa_prompt(q, k, v, qi, ki, iw)
        xp = finish_layer(xp, m_out, a_out, w_out[l], g_ffn[l], w_gate_up[l], w_down[l])
        kp.append(k); vp.append(v); kip.append(ki); Cp.append(C); Np.append(n); Mp.append(m); Bp.append(buf)
        mq, mk, mv, mo, mi, mf, aq, ak, av, iq, ik, iw = project(xs, g_mix[l], w_in[l])
        m_out, (C, n, m, buf) = mlstm_mixer(
            mq, mk, mv, mo, mi, mf, state_conv[l], state_C[l], state_n[l], state_m[l],
            conv_w[l], b_gate[l], g_mout[l])
        q, k, v, qi, ki = attn_inputs(aq, ak, av, iq, ik, pos_s, g_q[l], g_k[l])
        a_out = dsa_sample(q, k, v, qi, ki, iw, pos_s, l, cache_k, cache_v, cache_kidx, page_table)
        xs = finish_layer(xs, m_out, a_out, w_out[l], g_ffn[l], w_gate_up[l], w_down[l])
        ks_.append(k); vs_.append(v); kis.append(ki); Cs.append(C); Ns.append(n); Ms.append(m); Bs.append(buf)
    return (xp, xs,
            jnp.stack(kp), jnp.stack(vp), jnp.stack(kip), jnp.stack(Cp), jnp.stack(Np), jnp.stack(Mp), jnp.stack(Bp),
            jnp.stack(ks_), jnp.stack(vs_), jnp.stack(kis), jnp.stack(Cs), jnp.stack(Ns), jnp.stack(Ms), jnp.stack(Bs))
```

```python
import functools

import jax
import jax.numpy as jnp
from jax import lax
from jax.experimental import pallas as pl
from jax.experimental.pallas import tpu as pltpu

F32 = jnp.float32
BF16 = jnp.bfloat16
I32 = jnp.int32

D_MODEL = 1024
M_HEADS = 4
M_HEAD_DIM = 128
M_WIDTH = M_HEADS * M_HEAD_DIM
CONV_WIDTH = 4
A_HEAD_DIM = 64
A_HEADS = 8
A_KV_HEADS = 4
A_GROUP = A_HEADS // A_KV_HEADS
A_WIDTH = A_HEADS * A_HEAD_DIM
A_KV_WIDTH = A_KV_HEADS * A_HEAD_DIM
IDX_HEADS = 8
IDX_DIM = 64
TOPK_MAX = 256
PAGE_SIZE = 128
ROPE_THETA = 10000.0
D_FF = 2816
NORM_EPS = 1e-6

LANES = 128
SUBLANES = 8
INT_MIN = -(2 ** 31)
NEG_BIG = -1e30

PM_W = 4 * M_WIDTH
PA_W = A_WIDTH + 2 * A_KV_WIDTH
PI_W = IDX_HEADS * IDX_DIM + LANES
PG_W = LANES
MAIN_W = PM_W + PA_W + PI_W

VMEM_LIMIT = 56 * 1024 * 1024


def _cparams(sem):
    return pltpu.CompilerParams(dimension_semantics=sem, vmem_limit_bytes=VMEM_LIMIT)


def _split2(x):
    hi = x.astype(BF16)
    lo = (x - hi.astype(F32)).astype(BF16)
    return hi, lo


def _split3(x):
    a = x.astype(BF16)
    r = x - a.astype(F32)
    b = r.astype(BF16)
    c = (r - b.astype(F32)).astype(BF16)
    return a, b, c


def _dot(a, b):
    return jnp.dot(a, b, preferred_element_type=F32)


def _dot_nt(a, b):
    return lax.dot_general(a, b, (((1,), (1,)), ((), ())), preferred_element_type=F32)


def _proj_kernel(x_ref, g_ref, w_ref, wgh_ref, wgl_ref, pm_ref, pa_ref, pi_ref, pg_ref):
    x = x_ref[...]
    ms = jnp.mean(x * x, axis=-1, keepdims=True)
    y = x * lax.rsqrt(ms + NORM_EPS) * g_ref[...]
    yh, yl = _split2(y)
    pm_ref[...] = _dot(yh, w_ref[:, 0:PM_W])
    pa_ref[...] = _dot(yh, w_ref[:, PM_W:PM_W + PA_W])
    pi_ref[...] = _dot(yh, w_ref[:, PM_W + PA_W:MAIN_W])
    pg_ref[...] = _dot(yh, wgh_ref[...]) + _dot(yh, wgl_ref[...]) + _dot(yl, wgh_ref[...])


def _project(x, g, w_main, wg_hi, wg_lo):
    n = x.shape[0]
    tm = min(256, n)
    const = lambda i: (0, 0)
    return pl.pallas_call(
        _proj_kernel,
        grid=(n // tm,),
        in_specs=[
            pl.BlockSpec((tm, D_MODEL), lambda i: (i, 0)),
            pl.BlockSpec((1, D_MODEL), const),
            pl.BlockSpec((D_MODEL, MAIN_W), const),
            pl.BlockSpec((D_MODEL, PG_W), const),
            pl.BlockSpec((D_MODEL, PG_W), const),
        ],
        out_specs=[
            pl.BlockSpec((tm, PM_W), lambda i: (i, 0)),
            pl.BlockSpec((tm, PA_W), lambda i: (i, 0)),
            pl.BlockSpec((tm, PI_W), lambda i: (i, 0)),
            pl.BlockSpec((tm, PG_W), lambda i: (i, 0)),
        ],
        out_shape=[
            jax.ShapeDtypeStruct((n, PM_W), F32),
            jax.ShapeDtypeStruct((n, PA_W), F32),
            jax.ShapeDtypeStruct((n, PI_W), F32),
            jax.ShapeDtypeStruct((n, PG_W), F32),
        ],
        compiler_params=_cparams(("arbitrary",)),
        name="proj",
    )(x, g, w_main, wg_hi, wg_lo)


def _log_sigmoid(x):
    return jnp.minimum(x, 0.0) - jnp.log1p(jnp.exp(-jnp.abs(x)))


def _mlstm_kernel(*refs, lc, tb, tv, nc, has_state):
    if has_state:
        (pm_ref, pg_ref, cw_ref, bias_ref, gm_ref, c0_ref, n0_ref, m0_ref, conv0_ref,
         h_ref, c_out_ref, n_out_ref, m_out_ref, conv_out_ref,
         c_scr, n_scr, m_scr, ext_scr, u_scr) = refs
    else:
        (pm_ref, pg_ref, cw_ref, bias_ref, gm_ref,
         h_ref, c_out_ref, n_out_ref, m_out_ref, conv_out_ref,
         c_scr, n_scr, m_scr, ext_scr, u_scr) = refs
    c = pl.program_id(1)
    qk_w = 2 * M_WIDTH

    @pl.when(c == 0)
    def _init():
        ext_scr[0:SUBLANES, :] = jnp.zeros((SUBLANES, qk_w), F32)
        if has_state:
            c_scr[...] = c0_ref[0]
            n_scr[...] = n0_ref[0]
            m_scr[...] = m0_ref[0]
            ext_scr[SUBLANES - (CONV_WIDTH - 1):SUBLANES, :] = conv0_ref[0]
        else:
            c_scr[...] = jnp.zeros(c_scr.shape, F32)
            n_scr[...] = jnp.zeros(n_scr.shape, F32)
            m_scr[...] = jnp.zeros(m_scr.shape, F32)

    if tb < lc:
        u_scr[...] = jnp.zeros(u_scr.shape, F32)
        u_scr[0:tb, 0:PM_W] = pm_ref[0]
        u_scr[0:tb, PM_W:PM_W + PG_W] = pg_ref[0]
        pm = u_scr[:, 0:PM_W]
        gates = u_scr[:, PM_W:PM_W + PG_W]
    else:
        pm = pm_ref[0]
        gates = pg_ref[0]

    ext_scr[SUBLANES:SUBLANES + lc, :] = pm[:, 0:qk_w]
    cw = cw_ref[...]
    qk = ext_scr[pl.ds(SUBLANES - 3, lc), :] * cw[0:1, :]
    for j in range(1, CONV_WIDTH):
        qk = qk + ext_scr[pl.ds(SUBLANES - 3 + j, lc), :] * cw[j:j + 1, :]
    new_tail = ext_scr[pl.ds(tv, SUBLANES), :]
    qk = qk * jax.nn.sigmoid(qk)

    a = gates + bias_ref[...]
    ig = a
    lf = _log_sigmoid(a)
    if tv < lc:
        valid = lax.broadcasted_iota(I32, (lc, LANES), 0) < tv
        ig = jnp.where(valid, ig, NEG_BIG)
        lf = jnp.where(valid, lf, 0.0)
    ig_t = ig.T[0:SUBLANES, :]
    lf_t = lf.T[0:SUBLANES, :]

    r_i = lax.broadcasted_iota(I32, (lc, lc), 0)
    c_i = lax.broadcasted_iota(I32, (lc, lc), 1)
    causal = c_i <= r_i
    tril = jnp.where(causal, 1.0, 0.0).astype(BF16)
    triu = jnp.where(r_i <= c_i, 1.0, 0.0).astype(BF16)
    b_cols = sum(_dot(tril, p) for p in _split3(lf))
    b_rows = sum(_dot(p, triu) for p in _split3(lf_t))

    outs = []
    for h in range(M_HEADS):
        lo, hi = h * M_HEAD_DIM, (h + 1) * M_HEAD_DIM
        q = qk[:, lo:hi]
        k = qk[:, M_WIDTH + lo:M_WIDTH + hi] * (M_HEAD_DIM ** -0.5)
        v = pm[:, 2 * M_WIDTH + lo:2 * M_WIDTH + hi]
        og = pm[:, 3 * M_WIDTH + lo:3 * M_WIDTH + hi]
        b_col = b_cols[:, M_HEADS + h:M_HEADS + h + 1]
        i_col = ig[:, h:h + 1]
        b_row = b_rows[M_HEADS + h:M_HEADS + h + 1, :]
        i_row = ig_t[h:h + 1, :]
        m_prev = m_scr[h][:, 0:1]
        c_h = c_scr[h]
        n_h = n_scr[h]

        log_d = jnp.where(causal, b_col - b_row + i_row, NEG_BIG)
        inter = b_col + m_prev
        m_t = jnp.maximum(jnp.max(log_d, axis=-1, keepdims=True), inter)
        w_inter = jnp.exp(inter - m_t)
        d_mat = jnp.exp(log_d - m_t)
        qb = q.astype(BF16)
        s = _dot_nt(qb, k.astype(BF16)) * d_mat
        num = w_inter * _dot_nt(qb, c_h.astype(BF16)) + _dot(s.astype(BF16), v.astype(BF16))
        nq = w_inter * jnp.sum(q * n_h, axis=-1, keepdims=True) + jnp.sum(s, axis=-1, keepdims=True)
        hh = num / jnp.maximum(jnp.abs(nq), jnp.exp(-m_t))
        ms = jnp.mean(hh * hh, axis=-1, keepdims=True)
        hn = hh * lax.rsqrt(ms + NORM_EPS) * gm_ref[:, lo:hi]
        outs.append(jax.nn.sigmoid(og) * hn)

        b_last = b_cols[lc - 1:lc, M_HEADS + h:M_HEADS + h + 1]
        log_w = b_last - b_col + i_col
        m_new = jnp.maximum(b_last + m_prev, jnp.max(log_w, axis=0, keepdims=True))
        decay = jnp.exp(b_last + m_prev - m_new)
        kw = k * jnp.exp(log_w - m_new)
        c_scr[h] = decay * c_h + _dot(v.T.astype(BF16), kw.astype(BF16))
        n_scr[h] = decay * n_h + jnp.sum(kw, axis=0, keepdims=True)
        m_scr[h] = jnp.broadcast_to(m_new, (1, LANES))

    out = jnp.concatenate(outs, axis=-1)
    h_ref[0] = out[0:tb].astype(h_ref.dtype)
    ext_scr[0:SUBLANES, :] = new_tail

    @pl.when(c == nc - 1)
    def _fin():
        c_out_ref[0] = c_scr[...]
        n_out_ref[0] = n_scr[...]
        m_out_ref[0] = m_scr[...]
        conv_out_ref[0] = new_tail[SUBLANES - (CONV_WIDTH - 1):SUBLANES, :]


def _mlstm(pm, pg, conv_w, bias_row, g_mout, state):
    b, t, _ = pm.shape
    if t >= 256:
        lc, tb, tv = 256, 256, 256
    else:
        lc, tb, tv = LANES, t, t
    nc = max(t // lc, 1)
    has_state = state is not None
    qk_w = 2 * M_WIDTH
    const2 = lambda i, j: (0, 0)
    per_b4 = lambda i, j: (i, 0, 0, 0)
    in_specs = [
        pl.BlockSpec((1, tb, PM_W), lambda i, j: (i, j, 0)),
        pl.BlockSpec((1, tb, PG_W), lambda i, j: (i, j, 0)),
        pl.BlockSpec((CONV_WIDTH, qk_w), const2),
        pl.BlockSpec((1, PG_W), const2),
        pl.BlockSpec((1, M_WIDTH), const2),
    ]
    args = [pm, pg, conv_w, bias_row, g_mout]
    if has_state:
        in_specs += [
            pl.BlockSpec((1, M_HEADS, M_HEAD_DIM, M_HEAD_DIM), per_b4),
            pl.BlockSpec((1, M_HEADS, 1, M_HEAD_DIM), per_b4),
            pl.BlockSpec((1, M_HEADS, 1, LANES), per_b4),
            pl.BlockSpec((1, CONV_WIDTH - 1, qk_w), lambda i, j: (i, 0, 0)),
        ]
        args += list(state)
    out_specs = [
        pl.BlockSpec((1, tb, M_WIDTH), lambda i, j: (i, j, 0)),
        pl.BlockSpec((1, M_HEADS, M_HEAD_DIM, M_HEAD_DIM), per_b4),
        pl.BlockSpec((1, M_HEADS, 1, M_HEAD_DIM), per_b4),
        pl.BlockSpec((1, M_HEADS, 1, LANES), per_b4),
        pl.BlockSpec((1, CONV_WIDTH - 1, qk_w), lambda i, j: (i, 0, 0)),
    ]
    out_shape = [
        jax.ShapeDtypeStruct((b, t, M_WIDTH), BF16),
        jax.ShapeDtypeStruct((b, M_HEADS, M_HEAD_DIM, M_HEAD_DIM), F32),
        jax.ShapeDtypeStruct((b, M_HEADS, 1, M_HEAD_DIM), F32),
        jax.ShapeDtypeStruct((b, M_HEADS, 1, LANES), F32),
        jax.ShapeDtypeStruct((b, CONV_WIDTH - 1, qk_w), F32),
    ]
    scratch = [
        pltpu.VMEM((M_HEADS, M_HEAD_DIM, M_HEAD_DIM), F32),
        pltpu.VMEM((M_HEADS, 1, M_HEAD_DIM), F32),
        pltpu.VMEM((M_HEADS, 1, LANES), F32),
        pltpu.VMEM((lc + 2 * SUBLANES, qk_w), F32),
        pltpu.VMEM((lc, PM_W + PG_W), F32),
    ]
    kern = functools.partial(_mlstm_kernel, lc=lc, tb=tb, tv=tv, nc=nc, has_state=has_state)
    return pl.pallas_call(
        kern, grid=(b, nc), in_specs=in_specs, out_specs=out_specs, out_shape=out_shape,
        scratch_shapes=scratch, compiler_params=_cparams(("arbitrary", "arbitrary")), name="mlstm",
    )(*args)


def _swap_halves(x):
    lane = lax.broadcasted_iota(I32, x.shape, 1)
    first = (lane % A_HEAD_DIM) < (A_HEAD_DIM // 2)
    return jnp.where(first, pltpu.roll(x, LANES - A_HEAD_DIM // 2, 1), pltpu.roll(x, A_HEAD_DIM // 2, 1))


def _rope(x, cos, sin):
    return x * cos + _swap_halves(x) * sin


def _head_rms(x, seg, g):
    hi, lo = _split2(x * x)
    ss = _dot(hi, seg) + _dot(lo, seg)
    return x * lax.rsqrt(ss * (1.0 / A_HEAD_DIM) + NORM_EPS) * g


def _aprep_kernel(pa_ref, pi_ref, cos_ref, sin_ref, gq_ref, gk_ref, seg_ref,
                  q_ref, kh_ref, vd_ref, qi_ref, kib_ref, k_out_ref, v_out_ref, ki_out_ref):
    cos = cos_ref[...]
    sin = sin_ref[...]
    seg = seg_ref[...]
    lane = lax.broadcasted_iota(I32, cos.shape, 1)
    low = lane < A_HEAD_DIM
    half = A_HEAD_DIM

    def split_heads(blk):
        return blk[:, 0:half], pltpu.roll(blk, half, 1)[:, 0:half]

    for j in range(A_WIDTH // LANES):
        blk = _rope(_head_rms(pa_ref[0, :, j * LANES:(j + 1) * LANES], seg, gq_ref[...]), cos, sin)
        blk = blk * (A_HEAD_DIM ** -0.5)
        h0, h1 = split_heads(blk)
        q_ref[0, 2 * j] = h0.astype(BF16)
        q_ref[0, 2 * j + 1] = h1.astype(BF16)
    for j in range(A_KV_WIDTH // LANES):
        blk = _rope(_head_rms(pa_ref[0, :, A_WIDTH + j * LANES:A_WIDTH + (j + 1) * LANES], seg, gk_ref[...]), cos, sin)
        k_out_ref[0, :, j * LANES:(j + 1) * LANES] = blk
        h0, h1 = split_heads(blk)
        kh_ref[0, 2 * j] = h0.astype(BF16)
        kh_ref[0, 2 * j + 1] = h1.astype(BF16)
        vb = pa_ref[0, :, A_WIDTH + A_KV_WIDTH + j * LANES:A_WIDTH + A_KV_WIDTH + (j + 1) * LANES]
        v_out_ref[0, :, j * LANES:(j + 1) * LANES] = vb
        vr = pltpu.roll(vb, half, 1)
        vd_ref[0, 2 * j] = jnp.where(low, vb, vr).astype(BF16)
        vd_ref[0, 2 * j + 1] = jnp.where(low, vr, vb).astype(BF16)
    for j in range(IDX_HEADS * IDX_DIM // LANES):
        blk = _rope(pi_ref[0, :, j * LANES:(j + 1) * LANES], cos, sin) * (IDX_DIM ** -0.5)
        h0, h1 = split_heads(blk)
        qi_ref[0, 2 * j] = h0.astype(BF16)
        qi_ref[0, 2 * j + 1] = h1.astype(BF16)
    kblk = _rope(pi_ref[0, :, IDX_HEADS * IDX_DIM:IDX_HEADS * IDX_DIM + LANES], cos, sin)[:, 0:half]
    ki_out_ref[0] = kblk
    kib_ref[0] = kblk.astype(BF16)


def _aprep(pa, pi, cos, sin, gq, gk, seg):
    b, t, _ = pa.shape
    tm = min(256, t)
    c2 = lambda i, j: (0, 0)
    row3 = lambda i, j: (i, j, 0)
    hm = lambda i, j: (i, 0, j, 0)
    return pl.pallas_call(
        _aprep_kernel,
        grid=(b, t // tm),
        in_specs=[
            pl.BlockSpec((1, tm, PA_W), row3),
            pl.BlockSpec((1, tm, PI_W), row3),
            pl.BlockSpec((tm, LANES), lambda i, j: (j, 0)),
            pl.BlockSpec((tm, LANES), lambda i, j: (j, 0)),
            pl.BlockSpec((1, LANES), c2),
            pl.BlockSpec((1, LANES), c2),
            pl.BlockSpec((LANES, LANES), c2),
        ],
        out_specs=[
            pl.BlockSpec((1, A_HEADS, tm, A_HEAD_DIM), hm),
            pl.BlockSpec((1, A_KV_HEADS, tm, A_HEAD_DIM), hm),
            pl.BlockSpec((1, A_KV_HEADS, tm, LANES), hm),
            pl.BlockSpec((1, IDX_HEADS, tm, IDX_DIM), hm),
            pl.BlockSpec((1, tm, IDX_DIM), row3),
            pl.BlockSpec((1, tm, A_KV_WIDTH), row3),
            pl.BlockSpec((1, tm, A_KV_WIDTH), row3),
            pl.BlockSpec((1, tm, IDX_DIM), row3),
        ],
        out_shape=[
            jax.ShapeDtypeStruct((b, A_HEADS, t, A_HEAD_DIM), BF16),
            jax.ShapeDtypeStruct((b, A_KV_HEADS, t, A_HEAD_DIM), BF16),
            jax.ShapeDtypeStruct((b, A_KV_HEADS, t, LANES), BF16),
            jax.ShapeDtypeStruct((b, IDX_HEADS, t, IDX_DIM), BF16),
            jax.ShapeDtypeStruct((b, t, IDX_DIM), BF16),
            jax.ShapeDtypeStruct((b, t, A_KV_WIDTH), F32),
            jax.ShapeDtypeStruct((b, t, A_KV_WIDTH), F32),
            jax.ShapeDtypeStruct((b, t, IDX_DIM), F32),
        ],
        compiler_params=_cparams(("arbitrary", "arbitrary")),
        name="aprep",
    )(pa, pi, cos, sin, gq, gk, seg)


def _sortable_key(score):
    bits = lax.bitcast_convert_type(score, I32)
    key = bits ^ ((bits >> 31) & 0x7FFFFFFF)
    return jnp.where(key == -1, 0, key)


def _fold_lanes(x):
    acc = x[:, 0:LANES]
    for j in range(1, x.shape[1] // LANES):
        acc = acc + x[:, j * LANES:(j + 1) * LANES]
    return acc


def _kth_largest_key(keys_ref, nkb, rows, k_top):
    def count_ge(cand):
        def body(kb, acc):
            return acc + _fold_lanes((keys_ref[kb] >= cand).astype(I32))
        acc = lax.fori_loop(0, nkb, body, jnp.zeros((rows, LANES), I32))
        return jnp.sum(acc, axis=-1, keepdims=True)

    def bit_body(it, thr):
        cand = thr ^ lax.shift_left(jnp.int32(1), 31 - it)
        return jnp.where(count_ge(cand) >= k_top, cand, thr)

    thr = lax.fori_loop(0, 32, bit_body, jnp.full((rows, 1), INT_MIN, I32))
    return thr, count_ge


def _break_ties(keys_ref, nkb, rows, tk, thr, count_ge, k_top):
    cnt_ge = count_ge(thr)
    tie = jnp.logical_and(cnt_ge > k_top, thr > INT_MIN)

    @pl.when(jnp.max(tie.astype(I32)) > 0)
    def _fix():
        cnt_gt = cnt_ge - _count_eq(keys_ref, nkb, rows, thr)
        need = (k_top - cnt_gt).astype(F32)
        r_i = lax.broadcasted_iota(I32, (tk, tk), 0)
        c_i = lax.broadcasted_iota(I32, (tk, tk), 1)
        triu = jnp.where(r_i <= c_i, 1.0, 0.0).astype(BF16)

        def body(kb, run):
            key = keys_ref[kb]
            eq = key == thr
            pre = _dot(jnp.where(eq, 1.0, 0.0).astype(BF16), triu)
            drop = jnp.logical_and(eq, run + pre > need)
            keys_ref[kb] = jnp.where(drop, INT_MIN, key)
            return run + pre[:, tk - 1:tk]
        lax.fori_loop(0, nkb, body, jnp.zeros((rows, 1), F32))


def _count_eq(keys_ref, nkb, rows, thr):
    def body(kb, acc):
        return acc + _fold_lanes((keys_ref[kb] == thr).astype(I32))
    acc = lax.fori_loop(0, nkb, body, jnp.zeros((rows, LANES), I32))
    return jnp.sum(acc, axis=-1, keepdims=True)


def _dsa_prompt_kernel(q_ref, k_ref, v_ref, qi_ref, ki_ref, pg_ref, o_ref, keys_ref, *, tq, tk, k_top):
    i = pl.program_id(1)
    r0 = i * tq
    nkb = (r0 + tq + tk - 1) // tk
    row = r0 + lax.broadcasted_iota(I32, (tq, tk), 0)
    col0 = lax.broadcasted_iota(I32, (tq, tk), 1)

    qi = qi_ref[0].reshape(IDX_HEADS * tq, IDX_DIM)
    w = pg_ref[0][:, 2 * M_HEADS:2 * M_HEADS + IDX_HEADS] * (IDX_HEADS ** -0.5)

    def score_body(kb, carry):
        start = pl.multiple_of(kb * tk, tk)
        d = _dot_nt(qi, ki_ref[0, pl.ds(start, tk), :])
        d = jnp.maximum(d, 0.0).reshape(IDX_HEADS, tq, tk)
        sc = d[0] * w[:, 0:1]
        for h in range(1, IDX_HEADS):
            sc = sc + d[h] * w[:, h:h + 1]
        keys_ref[kb] = jnp.where(col0 + kb * tk <= row, _sortable_key(sc), INT_MIN)
        return carry
    lax.fori_loop(0, nkb, score_body, 0)

    thr, count_ge = _kth_largest_key(keys_ref, nkb, tq, k_top)
    _break_ties(keys_ref, nkb, tq, tk, thr, count_ge, k_top)
    thr = jnp.maximum(thr, INT_MIN + 1)

    lane = lax.broadcasted_iota(I32, (tq, LANES), 1)
    for g in range(A_KV_HEADS):
        q2 = q_ref[0, A_GROUP * g:A_GROUP * (g + 1)].reshape(A_GROUP * tq, A_HEAD_DIM)

        def att_body(kb, carry, g=g, q2=q2):
            m, l, acc = carry
            start = pl.multiple_of(kb * tk, tk)
            s = _dot_nt(q2, k_ref[0, g, pl.ds(start, tk), :])
            sel = keys_ref[kb] >= thr
            sel = jnp.concatenate([sel] * A_GROUP, axis=0)
            m_new = jnp.maximum(m, jnp.max(jnp.where(sel, s, NEG_BIG), axis=-1, keepdims=True))
            p = jnp.where(sel, jnp.exp(s - m_new), 0.0)
            alpha = jnp.exp(m - m_new)
            l = alpha * l + jnp.sum(p, axis=-1, keepdims=True)
            acc = alpha * acc + _dot(p.astype(BF16), v_ref[0, g, pl.ds(start, tk), :])
            return m_new, l, acc
        init = (jnp.full((A_GROUP * tq, 1), NEG_BIG, F32), jnp.zeros((A_GROUP * tq, 1), F32),
                jnp.zeros((A_GROUP * tq, LANES), F32))
        _, l, acc = lax.fori_loop(0, nkb, att_body, init)
        o = acc / l
        o_ref[0, :, g * LANES:(g + 1) * LANES] = jnp.where(lane < A_HEAD_DIM, o[0:tq], o[tq:2 * tq]).astype(o_ref.dtype)


def _dsa_prompt(q_hm, k_hm, v_dup, qi_hm, ki_bf, pg):
    b, _, s, _ = q_hm.shape
    tq = min(128, s)
    tk = min(512, s)
    k_top = min(TOPK_MAX, s // 4)
    qb = lambda i, j: (i, 0, j, 0)
    whole = lambda i, j: (i, 0, 0, 0)
    kern = functools.partial(_dsa_prompt_kernel, tq=tq, tk=tk, k_top=k_top)
    return pl.pallas_call(
        kern,
        grid=(b, s // tq),
        in_specs=[
            pl.BlockSpec((1, A_HEADS, tq, A_HEAD_DIM), qb),
            pl.BlockSpec((1, A_KV_HEADS, s, A_HEAD_DIM), whole),
            pl.BlockSpec((1, A_KV_HEADS, s, LANES), whole),
            pl.BlockSpec((1, IDX_HEADS, tq, IDX_DIM), qb),
            pl.BlockSpec((1, s, IDX_DIM), lambda i, j: (i, 0, 0)),
            pl.BlockSpec((1, tq, PG_W), lambda i, j: (i, j, 0)),
        ],
        out_specs=pl.BlockSpec((1, tq, A_WIDTH), lambda i, j: (i, j, 0)),
        out_shape=jax.ShapeDtypeStruct((b, s, A_WIDTH), BF16),
        scratch_shapes=[pltpu.VMEM((s // tk, tq, tk), I32)],
        compiler_params=_cparams(("arbitrary", "arbitrary")),
        name="dsa_prompt",
    )(q_hm, k_hm, v_dup, qi_hm, ki_bf, pg)


SAMPLE_ROWS = SUBLANES
PAGES_PER_STEP = 4


def _dsa_sample_kernel(pt_ref, q_ref, qi_ref, w_ref, kn_ref, vn_ref, kin_ref, *rest, n_steps, t_new, k_top):
    g = PAGES_PER_STEP
    kidx_refs, kc_refs, vc_refs = rest[0:g], rest[g:2 * g], rest[2 * g:3 * g]
    o_ref, keys_ref, thr_ref, m_ref, l_ref, acc_ref = rest[3 * g:]
    ph = pl.program_id(1)
    p = pl.program_id(2)
    rows = SAMPLE_ROWS
    n_blocks = n_steps * g + 1
    w = w_ref[0][:, 2 * M_HEADS:2 * M_HEADS + IDX_HEADS] * (IDX_HEADS ** -0.5)

    def scores(ki_blk):
        d = jnp.maximum(_dot_nt(qi_ref[0], ki_blk), 0.0).reshape(IDX_HEADS, rows, PAGE_SIZE)
        sc = d[0] * w[:, 0:1]
        for h in range(1, IDX_HEADS):
            sc = sc + d[h] * w[:, h:h + 1]
        return sc

    def attend(kblk, vblk, keyblk):
        s = _dot_nt(q_ref[0], kblk)
        sel = jnp.concatenate([keyblk >= thr_ref[...]] * A_HEADS, axis=0)
        m_old = m_ref[...]
        m_new = jnp.maximum(m_old, jnp.max(jnp.where(sel, s, NEG_BIG), axis=-1, keepdims=True))
        pr = jnp.where(sel, jnp.exp(s - m_new), 0.0)
        alpha = jnp.exp(m_old - m_new)
        m_ref[...] = m_new
        l_ref[...] = alpha * l_ref[...] + jnp.sum(pr, axis=-1, keepdims=True)
        acc_ref[...] = jnp.concatenate([alpha, alpha], axis=-1) * acc_ref[...] + _dot(pr.astype(BF16), vblk)

    @pl.when(ph == 0)
    def _index():
        for j in range(g):
            keys_ref[p * g + j] = _sortable_key(scores(kidx_refs[j][0, 0].astype(BF16)))

        @pl.when(p == n_steps - 1)
        def _select():
            t_i = lax.broadcasted_iota(I32, (rows, PAGE_SIZE), 0)
            j_i = lax.broadcasted_iota(I32, (rows, PAGE_SIZE), 1)
            vis = jnp.logical_and(j_i <= t_i, j_i < t_new)
            keys_ref[n_blocks - 1] = jnp.where(vis, _sortable_key(scores(kin_ref[0])), INT_MIN)
            thr, count_ge = _kth_largest_key(keys_ref, n_blocks, rows, k_top)
            _break_ties(keys_ref, n_blocks, rows, PAGE_SIZE, thr, count_ge, k_top)
            thr_ref[...] = jnp.broadcast_to(jnp.maximum(thr, INT_MIN + 1), (rows, LANES))
            m_ref[...] = jnp.full(m_ref.shape, NEG_BIG, F32)
            l_ref[...] = jnp.zeros(l_ref.shape, F32)
            acc_ref[...] = jnp.zeros(acc_ref.shape, F32)

    @pl.when(ph == 1)
    def _attend():
        for j in range(g):
            attend(kc_refs[j][0, 0].astype(BF16), vc_refs[j][0, 0].astype(BF16), keys_ref[p * g + j])

        @pl.when(p == n_steps - 1)
        def _emit():
            attend(kn_ref[0], vn_ref[0], keys_ref[n_blocks - 1])
            o = (acc_ref[...] / l_ref[...][:, 0:1]).astype(BF16)
            o_cat = jnp.concatenate([o[h * rows:(h + 1) * rows] for h in range(A_HEADS)], axis=-1)
            r_i = lax.broadcasted_iota(I32, (A_HEADS * A_KV_WIDTH, A_WIDTH), 0)
            c_i = lax.broadcasted_iota(I32, (A_HEADS * A_KV_WIDTH, A_WIDTH), 1)
            head = r_i // A_KV_WIDTH
            src = r_i % A_KV_WIDTH - (head // A_GROUP) * A_HEAD_DIM
            place = jnp.logical_and(c_i // A_HEAD_DIM == head, src == c_i % A_HEAD_DIM)
            o_ref[0] = _dot(o_cat, jnp.where(place, 1.0, 0.0).astype(BF16)).astype(o_ref.dtype)


def _dsa_sample(q_all, qi_all, w8, k_new, v_new, ki_new, cache_k, cache_v, cache_kidx, page_table, *, layer, t_new):
    db = q_all.shape[0]
    n_pages = page_table.shape[1]
    g = PAGES_PER_STEP
    n_steps = n_pages // g
    n_pool = cache_k.shape[1]
    ck = cache_k.reshape(cache_k.shape[0], n_pool, PAGE_SIZE, A_KV_WIDTH)
    cv = cache_v.reshape(cache_v.shape[0], n_pool, PAGE_SIZE, A_KV_WIDTH)
    k_top = min(TOPK_MAX, (n_pages * PAGE_SIZE + t_new) // 4)
    per_b = lambda b, ph, p, pt: (b, 0, 0)

    def page_map(phase, j):
        def index(b, ph, p, pt):
            parked = (n_steps - 1) * g + j if phase == 0 else j
            return (layer, pt[b * n_pages + jnp.where(ph == phase, p * g + j, parked)], 0, 0)
        return index

    in_specs = [
        pl.BlockSpec((1, A_HEADS * SAMPLE_ROWS, A_KV_WIDTH), per_b),
        pl.BlockSpec((1, IDX_HEADS * SAMPLE_ROWS, IDX_DIM), per_b),
        pl.BlockSpec((1, SAMPLE_ROWS, PG_W), per_b),
        pl.BlockSpec((1, PAGE_SIZE, A_KV_WIDTH), per_b),
        pl.BlockSpec((1, PAGE_SIZE, A_KV_WIDTH), per_b),
        pl.BlockSpec((1, PAGE_SIZE, IDX_DIM), per_b),
    ]
    in_specs += [pl.BlockSpec((1, 1, PAGE_SIZE, IDX_DIM), page_map(0, j)) for j in range(g)]
    in_specs += [pl.BlockSpec((1, 1, PAGE_SIZE, A_KV_WIDTH), page_map(1, j)) for j in range(g)]
    in_specs += [pl.BlockSpec((1, 1, PAGE_SIZE, A_KV_WIDTH), page_map(1, j)) for j in range(g)]
    rows_all = A_HEADS * SAMPLE_ROWS
    grid_spec = pltpu.PrefetchScalarGridSpec(
        num_scalar_prefetch=1,
        grid=(db, 2, n_steps),
        in_specs=in_specs,
        out_specs=pl.BlockSpec((1, SAMPLE_ROWS, A_WIDTH), per_b),
        scratch_shapes=[
            pltpu.VMEM((n_pages + 1, SAMPLE_ROWS, PAGE_SIZE), I32),
            pltpu.VMEM((SAMPLE_ROWS, LANES), I32),
            pltpu.VMEM((rows_all, LANES), F32),
            pltpu.VMEM((rows_all, LANES), F32),
            pltpu.VMEM((rows_all, A_KV_WIDTH), F32),
        ],
    )
    kern = functools.partial(_dsa_sample_kernel, n_steps=n_steps, t_new=t_new, k_top=k_top)
    return pl.pallas_call(
        kern, grid_spec=grid_spec,
        out_shape=jax.ShapeDtypeStruct((db, SAMPLE_ROWS, A_WIDTH), BF16),
        compiler_params=_cparams(("arbitrary", "arbitrary", "arbitrary")),
        name="dsa_sample",
    )(page_table.reshape(-1), q_all, qi_all, w8, k_new, v_new, ki_new,
      *([cache_kidx] * g), *([ck] * g), *([cv] * g))


def _sample_layouts(q_hm, qi_hm, k_o, v_o, ki_o, pg):
    db, _, t, _ = q_hm.shape
    pad_t = SAMPLE_ROWS - t
    qp = jnp.pad(q_hm, ((0, 0), (0, 0), (0, pad_t), (0, 0)))
    own = (jnp.arange(A_HEADS)[:, None] // A_GROUP == jnp.arange(A_KV_HEADS)[None, :]).astype(BF16)
    q_all = (qp[:, :, :, None, :] * own[None, :, None, :, None]).reshape(db, A_HEADS * SAMPLE_ROWS, A_KV_WIDTH)
    qi_all = jnp.pad(qi_hm, ((0, 0), (0, 0), (0, pad_t), (0, 0))).reshape(db, IDX_HEADS * SAMPLE_ROWS, IDX_DIM)
    w8 = jnp.pad(pg, ((0, 0), (0, pad_t), (0, 0)))
    pad_k = lambda x: jnp.pad(x, ((0, 0), (0, PAGE_SIZE - t), (0, 0))).astype(BF16)
    return q_all, qi_all, w8, pad_k(k_o), pad_k(v_o), pad_k(ki_o)


def _finish_kernel(x_ref, hm_ref, a_ref, wo_ref, gf_ref, wgu_ref, wd_ref, y_ref):
    x1 = (x_ref[...] + _dot(hm_ref[...], wo_ref[0:M_WIDTH, :]) + _dot(a_ref[...], wo_ref[M_WIDTH:M_WIDTH + A_WIDTH, :]))
    ms = jnp.mean(x1 * x1, axis=-1, keepdims=True)
    xn = (x1 * lax.rsqrt(ms + NORM_EPS) * gf_ref[...]).astype(BF16)
    g = _dot(xn, wgu_ref[:, 0:D_FF])
    u = _dot(xn, wgu_ref[:, D_FF:2 * D_FF])
    y_ref[...] = x1 + _dot((g * jax.nn.sigmoid(g) * u).astype(BF16), wd_ref[...])


def _finish(x, hm, a, w_out, g_ffn, w_gu, w_down):
    n = x.shape[0]
    tm = min(256, n)
    const = lambda i: (0, 0)
    row = lambda i: (i, 0)
    return pl.pallas_call(
        _finish_kernel,
        grid=(n // tm,),
        in_specs=[
            pl.BlockSpec((tm, D_MODEL), row),
            pl.BlockSpec((tm, M_WIDTH), row),
            pl.BlockSpec((tm, A_WIDTH), row),
            pl.BlockSpec((M_WIDTH + A_WIDTH, D_MODEL), const),
            pl.BlockSpec((1, D_MODEL), const),
            pl.BlockSpec((D_MODEL, 2 * D_FF), const),
            pl.BlockSpec((D_FF, D_MODEL), const),
        ],
        out_specs=pl.BlockSpec((tm, D_MODEL), row),
        out_shape=jax.ShapeDtypeStruct((n, D_MODEL), F32),
        compiler_params=_cparams(("arbitrary",)),
        name="finish",
    )(x, hm, a, w_out, g_ffn, w_gu, w_down)


def _rope_tables(pos):
    half = A_HEAD_DIM // 2
    inv = ROPE_THETA ** (-jnp.arange(half, dtype=F32) / half)
    ang = pos.astype(F32)[:, None] * inv[None, :]
    cos, sin = jnp.cos(ang), jnp.sin(ang)
    cos_t = jnp.tile(jnp.concatenate([cos, cos], axis=-1), (1, LANES // A_HEAD_DIM))
    sin_t = jnp.tile(jnp.concatenate([-sin, sin], axis=-1), (1, LANES // A_HEAD_DIM))
    return cos_t, sin_t


def _layer_weights(l, g_mix, w_in, conv_w, b_gate, g_mout, g_q, g_k, w_out, g_ffn, w_gate_up, w_down):
    w = w_in[l]
    o_mi = PM_W
    o_aq = o_mi + 2 * M_HEADS
    o_iw = o_aq + PA_W + IDX_HEADS * IDX_DIM + IDX_DIM
    w_main = jnp.concatenate(
        [w[:, 0:PM_W], w[:, o_aq:o_iw], jnp.zeros((D_MODEL, LANES - IDX_DIM), F32)], axis=1).astype(BF16)
    w_gate = jnp.concatenate(
        [w[:, o_mi:o_aq], w[:, o_iw:o_iw + IDX_HEADS], jnp.zeros((D_MODEL, PG_W - 2 * M_HEADS - IDX_HEADS), F32)], axis=1)
    wg_hi, wg_lo = _split2(w_gate)
    bias_row = jnp.concatenate([b_gate[l], jnp.zeros((PG_W - 2 * M_HEADS,), F32)])[None, :]
    tile2 = lambda g: jnp.tile(g, LANES // A_HEAD_DIM)[None, :]
    return dict(
        g_mix=g_mix[l][None, :], w_main=w_main, wg_hi=wg_hi, wg_lo=wg_lo,
        conv_w=conv_w[l], bias_row=bias_row, g_mout=g_mout[l][None, :],
        gq=tile2(g_q[l]), gk=tile2(g_k[l]),
        w_out=w_out[l].astype(BF16), g_ffn=g_ffn[l][None, :],
        w_gu=w_gate_up[l].astype(BF16), w_down=w_down[l].astype(BF16))


def kernel(x_prompt, x_sample, cache_k, cache_v, cache_kidx, page_table, state_C, state_n, state_m, state_conv,
           g_mix, w_in, conv_w, b_gate, g_mout, g_q, g_k, w_out, g_ffn, w_gate_up, w_down):
    b, s, _ = x_prompt.shape
    db, t, _ = x_sample.shape
    depth = w_in.shape[0]
    past = page_table.shape[1] * PAGE_SIZE
    cos_p, sin_p = _rope_tables(jnp.arange(s, dtype=I32))
    cos_s, sin_s = _rope_tables(past + jnp.arange(t, dtype=I32))
    seg = (jnp.arange(LANES)[:, None] // A_HEAD_DIM == jnp.arange(LANES)[None, :] // A_HEAD_DIM).astype(BF16)
    xp = x_prompt.reshape(b * s, D_MODEL)
    xs = x_sample.reshape(db * t, D_MODEL)
    kp, vp, kip, cp, np_, mp, bp = [], [], [], [], [], [], []
    ks_, vs_, kis, cs, ns, ms, bs = [], [], [], [], [], [], []
    for l in range(depth):
        wl = _layer_weights(l, g_mix, w_in, conv_w, b_gate, g_mout, g_q, g_k, w_out, g_ffn, w_gate_up, w_down)
        pm, pa, pi, pg = _project(xp, wl["g_mix"], wl["w_main"], wl["wg_hi"], wl["wg_lo"])
        pm3, pg3 = pm.reshape(b, s, PM_W), pg.reshape(b, s, PG_W)
        hm, c_o, n_o, m_o, conv_o = _mlstm(pm3, pg3, wl["conv_w"], wl["bias_row"], wl["g_mout"], None)
        q_hm, k_hm, v_dup, qi_hm, ki_bf, k_o, v_o, ki_o = _aprep(
            pa.reshape(b, s, PA_W), pi.reshape(b, s, PI_W), cos_p, sin_p, wl["gq"], wl["gk"], seg)
        a = _dsa_prompt(q_hm, k_hm, v_dup, qi_hm, ki_bf, pg3)
        xp = _finish(xp, hm.reshape(b * s, M_WIDTH), a.reshape(b * s, A_WIDTH),
                     wl["w_out"], wl["g_ffn"], wl["w_gu"], wl["w_down"])
        kp.append(k_o.reshape(b, s, A_KV_HEADS, A_HEAD_DIM))
        vp.append(v_o.reshape(b, s, A_KV_HEADS, A_HEAD_DIM))
        kip.append(ki_o)
        cp.append(c_o)
        np_.append(n_o[:, :, 0, :])
        mp.append(m_o[:, :, 0, 0])
        bp.append(conv_o)

        pm, pa, pi, pg = _project(xs, wl["g_mix"], wl["w_main"], wl["wg_hi"], wl["wg_lo"])
        pg3 = pg.reshape(db, t, PG_W)
        state = (state_C[l], state_n[l][:, :, None, :],
                 jnp.broadcast_to(state_m[l][:, :, None, None], (db, M_HEADS, 1, LANES)), state_conv[l])
        hm, c_o, n_o, m_o, conv_o = _mlstm(pm.reshape(db, t, PM_W), pg3, wl["conv_w"], wl["bias_row"],
                                           wl["g_mout"], state)
        q_hm, _, _, qi_hm, _, k_o, v_o, ki_o = _aprep(
            pa.reshape(db, t, PA_W), pi.reshape(db, t, PI_W), cos_s, sin_s, wl["gq"], wl["gk"], seg)
        a = _dsa_sample(*_sample_layouts(q_hm, qi_hm, k_o, v_o, ki_o, pg3),
                        cache_k, cache_v, cache_kidx, page_table, layer=l, t_new=t)
        xs = _finish(xs, hm.reshape(db * t, M_WIDTH), a[:, 0:t].reshape(db * t, A_WIDTH),
                     wl["w_out"], wl["g_ffn"], wl["w_gu"], wl["w_down"])
        ks_.append(k_o.reshape(db, t, A_KV_HEADS, A_HEAD_DIM))
        vs_.append(v_o.reshape(db, t, A_KV_HEADS, A_HEAD_DIM))
        kis.append(ki_o)
        cs.append(c_o)
        ns.append(n_o[:, :, 0, :])
        ms.append(m_o[:, :, 0, 0])
        bs.append(conv_o)
    return (xp.reshape(b, s, D_MODEL), xs.reshape(db, t, D_MODEL),
            jnp.stack(kp), jnp.stack(vp), jnp.stack(kip), jnp.stack(cp), jnp.stack(np_), jnp.stack(mp), jnp.stack(bp),
            jnp.stack(ks_), jnp.stack(vs_), jnp.stack(kis), jnp.stack(cs), jnp.stack(ns), jnp.stack(ms), jnp.stack(bs))
```

```python
import functools
import math

import jax
import jax.numpy as jnp
from jax import lax
from jax.experimental import pallas as pl
from jax.experimental.pallas import tpu as pltpu

F32 = jnp.float32
BF16 = jnp.bfloat16
I32 = jnp.int32

D_MODEL = 1024
M_HEADS = 4
M_HEAD_DIM = 128
M_WIDTH = M_HEADS * M_HEAD_DIM
CONV_WIDTH = 4
A_HEAD_DIM = 64
A_HEADS = 8
A_KV_HEADS = 4
A_GROUP = A_HEADS // A_KV_HEADS
A_WIDTH = A_HEADS * A_HEAD_DIM
A_KV_WIDTH = A_KV_HEADS * A_HEAD_DIM
IDX_HEADS = 8
IDX_DIM = 64
TOPK_MAX = 256
PAGE_SIZE = 128
ROPE_THETA = 10000.0
D_FF = 2816
NORM_EPS = 1e-6

LANES = 128
SUBLANES = 8
INT_MIN = -(2 ** 31)
NEG_BIG = -1e30
F32_LOWEST = -3.0e38
LOG2E = math.log2(math.e)

PM_W = 4 * M_WIDTH
PA_W = A_WIDTH + 2 * A_KV_WIDTH
PI_W = IDX_HEADS * IDX_DIM + LANES
PG_W = LANES
MAIN_W = PM_W + PA_W + PI_W

ROW_TILE = 256
MLSTM_CHUNK = 256
DSA_TQ = 128
DSA_TK = 512
SAMPLE_ROWS = SUBLANES
PAGES_PER_STEP = 16
N_BISECT = 24
VMEM_LIMIT = 56 * 1024 * 1024


def _cparams(sem):
    return pltpu.CompilerParams(dimension_semantics=sem, vmem_limit_bytes=VMEM_LIMIT)


def _split2(x):
    hi = x.astype(BF16)
    lo = (x - hi.astype(F32)).astype(BF16)
    return hi, lo


def _split3(x):
    a = x.astype(BF16)
    r = x - a.astype(F32)
    b = r.astype(BF16)
    c = (r - b.astype(F32)).astype(BF16)
    return a, b, c


def _dot(a, b):
    return jnp.dot(a, b, preferred_element_type=F32)


def _dot_nt(a, b):
    return lax.dot_general(a, b, (((1,), (1,)), ((), ())), preferred_element_type=F32)


def _proj_kernel(x_ref, g_ref, w_ref, wgh_ref, wgl_ref, pm_ref, pa_ref, pi_ref, pg_ref):
    x = x_ref[...]
    ms = jnp.mean(x * x, axis=-1, keepdims=True)
    y = x * lax.rsqrt(ms + NORM_EPS) * g_ref[...]
    yh, yl = _split2(y)
    pm_ref[...] = _dot(yh, w_ref[:, 0:PM_W])
    pa_ref[...] = _dot(yh, w_ref[:, PM_W:PM_W + PA_W])
    pi_ref[...] = _dot(yh, w_ref[:, PM_W + PA_W:MAIN_W])
    pg_ref[...] = _dot(yh, wgh_ref[...]) + _dot(yh, wgl_ref[...]) + _dot(yl, wgh_ref[...])


def _project(x, g, w_main, wg_hi, wg_lo):
    n = x.shape[0]
    tm = min(ROW_TILE, n)
    const = lambda i: (0, 0)
    return pl.pallas_call(
        _proj_kernel,
        grid=(n // tm,),
        in_specs=[
            pl.BlockSpec((tm, D_MODEL), lambda i: (i, 0)),
            pl.BlockSpec((1, D_MODEL), const),
            pl.BlockSpec((D_MODEL, MAIN_W), const),
            pl.BlockSpec((D_MODEL, PG_W), const),
            pl.BlockSpec((D_MODEL, PG_W), const),
        ],
        out_specs=[
            pl.BlockSpec((tm, PM_W), lambda i: (i, 0)),
            pl.BlockSpec((tm, PA_W), lambda i: (i, 0)),
            pl.BlockSpec((tm, PI_W), lambda i: (i, 0)),
            pl.BlockSpec((tm, PG_W), lambda i: (i, 0)),
        ],
        out_shape=[
            jax.ShapeDtypeStruct((n, PM_W), F32),
            jax.ShapeDtypeStruct((n, PA_W), F32),
            jax.ShapeDtypeStruct((n, PI_W), F32),
            jax.ShapeDtypeStruct((n, PG_W), F32),
        ],
        compiler_params=_cparams(("arbitrary",)),
        name="proj",
    )(x, g, w_main, wg_hi, wg_lo)


def _log_sigmoid(x):
    return jnp.minimum(x, 0.0) - jnp.log1p(jnp.exp(-jnp.abs(x)))


def _mlstm_kernel(*refs, lc, tb, tv, nc, has_state):
    if has_state:
        (pm_ref, pg_ref, cw_ref, bias_ref, gm_ref, c0_ref, n0_ref, m0_ref, conv0_ref,
         h_ref, c_out_ref, n_out_ref, m_out_ref, conv_out_ref,
         c_scr, n_scr, m_scr, ext_scr, u_scr) = refs
    else:
        (pm_ref, pg_ref, cw_ref, bias_ref, gm_ref,
         h_ref, c_out_ref, n_out_ref, m_out_ref, conv_out_ref,
         c_scr, n_scr, m_scr, ext_scr, u_scr) = refs
    c = pl.program_id(1)
    qk_w = 2 * M_WIDTH

    @pl.when(c == 0)
    def _init():
        ext_scr[0:SUBLANES, :] = jnp.zeros((SUBLANES, qk_w), F32)
        if has_state:
            c_scr[...] = c0_ref[0]
            n_scr[...] = n0_ref[0]
            m_scr[...] = m0_ref[0]
            ext_scr[SUBLANES - (CONV_WIDTH - 1):SUBLANES, :] = conv0_ref[0]
        else:
            c_scr[...] = jnp.zeros(c_scr.shape, F32)
            n_scr[...] = jnp.zeros(n_scr.shape, F32)
            m_scr[...] = jnp.zeros(m_scr.shape, F32)

    if tb < lc:
        u_scr[...] = jnp.zeros(u_scr.shape, F32)
        u_scr[0:tb, 0:PM_W] = pm_ref[0]
        u_scr[0:tb, PM_W:PM_W + PG_W] = pg_ref[0]
        pm = u_scr[:, 0:PM_W]
        gates = u_scr[:, PM_W:PM_W + PG_W]
    else:
        pm = pm_ref[0]
        gates = pg_ref[0]

    ext_scr[SUBLANES:SUBLANES + lc, :] = pm[:, 0:qk_w]
    cw = cw_ref[...]
    qk = ext_scr[pl.ds(SUBLANES - 3, lc), :] * cw[0:1, :]
    for j in range(1, CONV_WIDTH):
        qk = qk + ext_scr[pl.ds(SUBLANES - 3 + j, lc), :] * cw[j:j + 1, :]
    new_tail = ext_scr[pl.ds(tv, SUBLANES), :]
    qk = qk * jax.nn.sigmoid(qk)

    a = gates + bias_ref[...]
    ig = a
    lf = _log_sigmoid(a)
    if tv < lc:
        valid = lax.broadcasted_iota(I32, (lc, LANES), 0) < tv
        ig = jnp.where(valid, ig, NEG_BIG)
        lf = jnp.where(valid, lf, 0.0)
    ig_t = ig.T[0:SUBLANES, :]
    lf_t = lf.T[0:SUBLANES, :]

    r_i = lax.broadcasted_iota(I32, (lc, lc), 0)
    c_i = lax.broadcasted_iota(I32, (lc, lc), 1)
    causal = c_i <= r_i
    tril = jnp.where(causal, 1.0, 0.0).astype(BF16)
    triu = jnp.where(r_i <= c_i, 1.0, 0.0).astype(BF16)
    b_cols = sum(_dot(tril, p) for p in _split3(lf))
    b_rows = sum(_dot(p, triu) for p in _split3(lf_t))

    outs = []
    for h in range(M_HEADS):
        lo, hi = h * M_HEAD_DIM, (h + 1) * M_HEAD_DIM
        q = qk[:, lo:hi]
        k = qk[:, M_WIDTH + lo:M_WIDTH + hi] * (M_HEAD_DIM ** -0.5)
        v = pm[:, 2 * M_WIDTH + lo:2 * M_WIDTH + hi]
        og = pm[:, 3 * M_WIDTH + lo:3 * M_WIDTH + hi]
        b_col = b_cols[:, M_HEADS + h:M_HEADS + h + 1]
        i_col = ig[:, h:h + 1]
        b_row = b_rows[M_HEADS + h:M_HEADS + h + 1, :]
        i_row = ig_t[h:h + 1, :]
        m_prev = m_scr[h][:, 0:1]
        c_h = c_scr[h]
        n_h = n_scr[h]

        log_d = jnp.where(causal, b_col - b_row + i_row, NEG_BIG)
        inter = b_col + m_prev
        m_t = jnp.maximum(jnp.max(log_d, axis=-1, keepdims=True), inter)
        w_inter = jnp.exp(inter - m_t)
        d_mat = jnp.exp(log_d - m_t)
        qb = q.astype(BF16)
        s = _dot_nt(qb, k.astype(BF16)) * d_mat
        num = w_inter * _dot_nt(qb, c_h.astype(BF16)) + _dot(s.astype(BF16), v.astype(BF16))
        nq = w_inter * jnp.sum(q * n_h, axis=-1, keepdims=True) + jnp.sum(s, axis=-1, keepdims=True)
        hh = num / jnp.maximum(jnp.abs(nq), jnp.exp(-m_t))
        ms = jnp.mean(hh * hh, axis=-1, keepdims=True)
        hn = hh * lax.rsqrt(ms + NORM_EPS) * gm_ref[:, lo:hi]
        outs.append(jax.nn.sigmoid(og) * hn)

        b_last = b_cols[lc - 1:lc, M_HEADS + h:M_HEADS + h + 1]
        log_w = b_last - b_col + i_col
        m_new = jnp.maximum(b_last + m_prev, jnp.max(log_w, axis=0, keepdims=True))
        decay = jnp.exp(b_last + m_prev - m_new)
        kw = k * jnp.exp(log_w - m_new)
        c_scr[h] = decay * c_h + _dot(v.T.astype(BF16), kw.astype(BF16))
        n_scr[h] = decay * n_h + jnp.sum(kw, axis=0, keepdims=True)
        m_scr[h] = jnp.broadcast_to(m_new, (1, LANES))

    out = jnp.concatenate(outs, axis=-1)
    h_ref[0] = out[0:tb].astype(h_ref.dtype)
    ext_scr[0:SUBLANES, :] = new_tail

    @pl.when(c == nc - 1)
    def _fin():
        c_out_ref[0] = c_scr[...]
        n_out_ref[0] = n_scr[...]
        m_out_ref[0] = m_scr[...]
        conv_out_ref[0] = new_tail[SUBLANES - (CONV_WIDTH - 1):SUBLANES, :]


def _mlstm(pm, pg, conv_w, bias_row, g_mout, state):
    b, t, _ = pm.shape
    if t >= MLSTM_CHUNK:
        lc, tb, tv = MLSTM_CHUNK, MLSTM_CHUNK, MLSTM_CHUNK
    else:
        lc, tb, tv = LANES, t, t
    nc = max(t // lc, 1)
    has_state = state is not None
    qk_w = 2 * M_WIDTH
    const2 = lambda i, j: (0, 0)
    per_b4 = lambda i, j: (i, 0, 0, 0)
    in_specs = [
        pl.BlockSpec((1, tb, PM_W), lambda i, j: (i, j, 0)),
        pl.BlockSpec((1, tb, PG_W), lambda i, j: (i, j, 0)),
        pl.BlockSpec((CONV_WIDTH, qk_w), const2),
        pl.BlockSpec((1, PG_W), const2),
        pl.BlockSpec((1, M_WIDTH), const2),
    ]
    args = [pm, pg, conv_w, bias_row, g_mout]
    if has_state:
        in_specs += [
            pl.BlockSpec((1, M_HEADS, M_HEAD_DIM, M_HEAD_DIM), per_b4),
            pl.BlockSpec((1, M_HEADS, 1, M_HEAD_DIM), per_b4),
            pl.BlockSpec((1, M_HEADS, 1, LANES), per_b4),
            pl.BlockSpec((1, CONV_WIDTH - 1, qk_w), lambda i, j: (i, 0, 0)),
        ]
        args += list(state)
    out_specs = [
        pl.BlockSpec((1, tb, M_WIDTH), lambda i, j: (i, j, 0)),
        pl.BlockSpec((1, M_HEADS, M_HEAD_DIM, M_HEAD_DIM), per_b4),
        pl.BlockSpec((1, M_HEADS, 1, M_HEAD_DIM), per_b4),
        pl.BlockSpec((1, M_HEADS, 1, LANES), per_b4),
        pl.BlockSpec((1, CONV_WIDTH - 1, qk_w), lambda i, j: (i, 0, 0)),
    ]
    out_shape = [
        jax.ShapeDtypeStruct((b, t, M_WIDTH), BF16),
        jax.ShapeDtypeStruct((b, M_HEADS, M_HEAD_DIM, M_HEAD_DIM), F32),
        jax.ShapeDtypeStruct((b, M_HEADS, 1, M_HEAD_DIM), F32),
        jax.ShapeDtypeStruct((b, M_HEADS, 1, LANES), F32),
        jax.ShapeDtypeStruct((b, CONV_WIDTH - 1, qk_w), F32),
    ]
    scratch = [
        pltpu.VMEM((M_HEADS, M_HEAD_DIM, M_HEAD_DIM), F32),
        pltpu.VMEM((M_HEADS, 1, M_HEAD_DIM), F32),
        pltpu.VMEM((M_HEADS, 1, LANES), F32),
        pltpu.VMEM((lc + 2 * SUBLANES, qk_w), F32),
        pltpu.VMEM((lc, PM_W + PG_W), F32),
    ]
    kern = functools.partial(_mlstm_kernel, lc=lc, tb=tb, tv=tv, nc=nc, has_state=has_state)
    return pl.pallas_call(
        kern, grid=(b, nc), in_specs=in_specs, out_specs=out_specs, out_shape=out_shape,
        scratch_shapes=scratch, compiler_params=_cparams(("arbitrary", "arbitrary")), name="mlstm",
    )(*args)


def _swap_halves(x):
    lane = lax.broadcasted_iota(I32, x.shape, 1)
    first = (lane % A_HEAD_DIM) < (A_HEAD_DIM // 2)
    return jnp.where(first, pltpu.roll(x, LANES - A_HEAD_DIM // 2, 1), pltpu.roll(x, A_HEAD_DIM // 2, 1))


def _rope(x, cos, sin):
    return x * cos + _swap_halves(x) * sin


def _head_rms(x, seg, g):
    hi, lo = _split2(x * x)
    ss = _dot(hi, seg) + _dot(lo, seg)
    return x * lax.rsqrt(ss * (1.0 / A_HEAD_DIM) + NORM_EPS) * g


def _aprep_kernel(pa_ref, pi_ref, cos_ref, sin_ref, gq_ref, gk_ref, seg_ref,
                  q_ref, kh_ref, vd_ref, qi_ref, kib_ref, k_out_ref, v_out_ref, ki_out_ref):
    cos = cos_ref[...]
    sin = sin_ref[...]
    seg = seg_ref[...]
    lane = lax.broadcasted_iota(I32, cos.shape, 1)
    low = lane < A_HEAD_DIM
    half = A_HEAD_DIM

    def split_heads(blk):
        return blk[:, 0:half], pltpu.roll(blk, half, 1)[:, 0:half]

    for j in range(A_WIDTH // LANES):
        blk = _rope(_head_rms(pa_ref[0, :, j * LANES:(j + 1) * LANES], seg, gq_ref[...]), cos, sin)
        blk = blk * (A_HEAD_DIM ** -0.5 * LOG2E)
        h0, h1 = split_heads(blk)
        q_ref[0, 2 * j] = h0.astype(BF16)
        q_ref[0, 2 * j + 1] = h1.astype(BF16)
    for j in range(A_KV_WIDTH // LANES):
        blk = _rope(_head_rms(pa_ref[0, :, A_WIDTH + j * LANES:A_WIDTH + (j + 1) * LANES], seg, gk_ref[...]), cos, sin)
        k_out_ref[0, :, j * LANES:(j + 1) * LANES] = blk
        h0, h1 = split_heads(blk)
        kh_ref[0, 2 * j] = h0.astype(BF16)
        kh_ref[0, 2 * j + 1] = h1.astype(BF16)
        vb = pa_ref[0, :, A_WIDTH + A_KV_WIDTH + j * LANES:A_WIDTH + A_KV_WIDTH + (j + 1) * LANES]
        v_out_ref[0, :, j * LANES:(j + 1) * LANES] = vb
        vr = pltpu.roll(vb, half, 1)
        vd_ref[0, 2 * j] = jnp.where(low, vb, vr).astype(BF16)
        vd_ref[0, 2 * j + 1] = jnp.where(low, vr, vb).astype(BF16)
    for j in range(IDX_HEADS * IDX_DIM // LANES):
        blk = _rope(pi_ref[0, :, j * LANES:(j + 1) * LANES], cos, sin) * (IDX_DIM ** -0.5)
        h0, h1 = split_heads(blk)
        qi_ref[0, 2 * j] = h0.astype(BF16)
        qi_ref[0, 2 * j + 1] = h1.astype(BF16)
    kblk = _rope(pi_ref[0, :, IDX_HEADS * IDX_DIM:IDX_HEADS * IDX_DIM + LANES], cos, sin)[:, 0:half]
    ki_out_ref[0] = kblk
    kib_ref[0] = kblk.astype(BF16)


def _aprep(pa, pi, cos, sin, gq, gk, seg):
    b, t, _ = pa.shape
    tm = min(ROW_TILE, t)
    c2 = lambda i, j: (0, 0)
    row3 = lambda i, j: (i, j, 0)
    hm = lambda i, j: (i, 0, j, 0)
    return pl.pallas_call(
        _aprep_kernel,
        grid=(b, t // tm),
        in_specs=[
            pl.BlockSpec((1, tm, PA_W), row3),
            pl.BlockSpec((1, tm, PI_W), row3),
            pl.BlockSpec((tm, LANES), lambda i, j: (j, 0)),
            pl.BlockSpec((tm, LANES), lambda i, j: (j, 0)),
            pl.BlockSpec((1, LANES), c2),
            pl.BlockSpec((1, LANES), c2),
            pl.BlockSpec((LANES, LANES), c2),
        ],
        out_specs=[
            pl.BlockSpec((1, A_HEADS, tm, A_HEAD_DIM), hm),
            pl.BlockSpec((1, A_KV_HEADS, tm, A_HEAD_DIM), hm),
            pl.BlockSpec((1, A_KV_HEADS, tm, LANES), hm),
            pl.BlockSpec((1, IDX_HEADS, tm, IDX_DIM), hm),
            pl.BlockSpec((1, tm, IDX_DIM), row3),
            pl.BlockSpec((1, tm, A_KV_WIDTH), row3),
            pl.BlockSpec((1, tm, A_KV_WIDTH), row3),
            pl.BlockSpec((1, tm, IDX_DIM), row3),
        ],
        out_shape=[
            jax.ShapeDtypeStruct((b, A_HEADS, t, A_HEAD_DIM), BF16),
            jax.ShapeDtypeStruct((b, A_KV_HEADS, t, A_HEAD_DIM), BF16),
            jax.ShapeDtypeStruct((b, A_KV_HEADS, t, LANES), BF16),
            jax.ShapeDtypeStruct((b, IDX_HEADS, t, IDX_DIM), BF16),
            jax.ShapeDtypeStruct((b, t, IDX_DIM), BF16),
            jax.ShapeDtypeStruct((b, t, A_KV_WIDTH), F32),
            jax.ShapeDtypeStruct((b, t, A_KV_WIDTH), F32),
            jax.ShapeDtypeStruct((b, t, IDX_DIM), F32),
        ],
        compiler_params=_cparams(("arbitrary", "arbitrary")),
        name="aprep",
    )(pa, pi, cos, sin, gq, gk, seg)


def _sortable_key(score):
    bits = lax.bitcast_convert_type(score, I32)
    key = bits ^ ((bits >> 31) & 0x7FFFFFFF)
    key = jnp.where(key == -1, 0, key)
    return jnp.where(score == -jnp.inf, INT_MIN, key)


def _fold_lanes(x):
    acc = x[:, 0:LANES]
    for j in range(1, x.shape[1] // LANES):
        acc = acc + x[:, j * LANES:(j + 1) * LANES]
    return acc


def _row_count(mask):
    return jnp.sum(_fold_lanes(mask.astype(I32)), axis=-1, keepdims=True)


def _bisect(count_ge, lo, hi, k_top):
    def body(_, c):
        lo, hi = c
        mid = lo + (hi - lo) * 0.5
        ge = count_ge(mid) >= k_top
        return jnp.where(ge, mid, lo), jnp.where(ge, hi, mid)
    return lax.fori_loop(0, N_BISECT, body, (lo, hi))[0]


def _kth_key(count_ge_key, rows, k_top):
    def body(it, thr):
        cand = thr ^ lax.shift_left(jnp.int32(1), 31 - it)
        return jnp.where(count_ge_key(cand) >= k_top, cand, thr)
    return lax.fori_loop(0, 32, body, jnp.full((rows, 1), INT_MIN, I32))


def _tie_rank(eq, before):
    n = eq.shape[1]
    r_i = lax.broadcasted_iota(I32, (n, n), 0)
    c_i = lax.broadcasted_iota(I32, (n, n), 1)
    triu = jnp.where(r_i <= c_i, 1.0, 0.0).astype(BF16)
    return before + _dot(jnp.where(eq, 1.0, 0.0).astype(BF16), triu)


def _select_dense(sc, k_top, bias_ref):
    rows, n = sc.shape
    lo0 = jnp.min(jnp.where(sc == -jnp.inf, jnp.inf, sc), axis=-1, keepdims=True)
    hi0 = jnp.max(sc, axis=-1, keepdims=True)
    count_ge = lambda thr: _row_count(sc >= thr)
    lo = _bisect(count_ge, lo0, hi0, k_top)
    bias_ref[...] = jnp.where(sc >= lo, 0.0, NEG_BIG)

    @pl.when(jnp.max(count_ge(lo)) > k_top)
    def _exact():
        key = _sortable_key(sc)
        thr = _kth_key(lambda c: _row_count(key >= c), rows, k_top)
        eq = key == thr
        need = (k_top - (_row_count(key >= thr) - _row_count(eq))).astype(F32)
        before = jnp.zeros((rows, 1), F32)
        ranks = []
        for j in range(n // LANES):
            eq_j = eq[:, j * LANES:(j + 1) * LANES]
            ranks.append(_tie_rank(eq_j, before))
            before = before + _row_count(eq_j).astype(F32)
        keep_eq = jnp.logical_and(eq, jnp.logical_and(jnp.concatenate(ranks, axis=-1) <= need, thr > INT_MIN))
        bias_ref[...] = jnp.where(jnp.logical_or(key > thr, keep_eq), 0.0, NEG_BIG)


def _dsa_prompt_kernel(q_ref, k_ref, v_ref, qi_ref, ki_ref, pg_ref, o_ref, sc_ref, thr_ref, *, tq, tk, k_top):
    i = pl.program_id(1)
    r0 = i * tq
    nkb = (r0 + tq + tk - 1) // tk
    row = r0 + lax.broadcasted_iota(I32, (tq, tk), 0)
    col0 = lax.broadcasted_iota(I32, (tq, tk), 1)

    qi = qi_ref[0].reshape(IDX_HEADS * tq, IDX_DIM)
    w = pg_ref[0][:, 2 * M_HEADS:2 * M_HEADS + IDX_HEADS] * (IDX_HEADS ** -0.5)

    def score_body(kb, carry):
        lo, hi = carry
        start = pl.multiple_of(kb * tk, tk)
        d = _dot_nt(qi, ki_ref[0, pl.ds(start, tk), :])
        d = jnp.maximum(d, 0.0).reshape(IDX_HEADS, tq, tk)
        sc = d[0] * w[:, 0:1]
        for h in range(1, IDX_HEADS):
            sc = sc + d[h] * w[:, h:h + 1]
        vis = col0 + kb * tk <= row
        sc_ref[kb] = jnp.where(vis, sc, -jnp.inf)
        lo = jnp.minimum(lo, jnp.min(jnp.where(vis, sc, jnp.inf), axis=-1, keepdims=True))
        hi = jnp.maximum(hi, jnp.max(jnp.where(vis, sc, -jnp.inf), axis=-1, keepdims=True))
        return lo, hi
    lo0, hi0 = lax.fori_loop(0, nkb, score_body,
                             (jnp.full((tq, 1), jnp.inf, F32), jnp.full((tq, 1), -jnp.inf, F32)))

    def count_blocks(pred):
        def body(kb, acc):
            return acc + _fold_lanes(pred(sc_ref[kb]).astype(I32))
        acc = lax.fori_loop(0, nkb, body, jnp.zeros((tq, LANES), I32))
        return jnp.sum(acc, axis=-1, keepdims=True)

    count_ge = lambda thr: count_blocks(lambda sc: sc >= thr)
    lo = _bisect(count_ge, lo0, hi0, k_top)
    thr_ref[...] = jnp.broadcast_to(lo, (tq, LANES))

    @pl.when(jnp.max(count_ge(lo)) > k_top)
    def _exact():
        thr = _kth_key(lambda c: count_blocks(lambda sc: _sortable_key(sc) >= c), tq, k_top)
        n_ge = count_blocks(lambda sc: _sortable_key(sc) >= thr)
        n_eq = count_blocks(lambda sc: _sortable_key(sc) == thr)
        need = (k_top - (n_ge - n_eq)).astype(F32)

        def body(kb, before):
            sc = sc_ref[kb]
            key = _sortable_key(sc)
            eq = key == thr
            keep_eq = jnp.logical_and(eq, jnp.logical_and(_tie_rank(eq, before) <= need, thr > INT_MIN))
            sc_ref[kb] = jnp.where(jnp.logical_or(key > thr, keep_eq), sc, -jnp.inf)
            return before + _row_count(eq).astype(F32)
        lax.fori_loop(0, nkb, body, jnp.zeros((tq, 1), F32))
        thr_ref[...] = jnp.full((tq, LANES), F32_LOWEST, F32)

    thr = thr_ref[...][:, 0:1]
    rows2 = A_GROUP * tq

    def att_body(kb, carry):
        start = pl.multiple_of(kb * tk, tk)
        bias = jnp.where(sc_ref[kb] >= thr, 0.0, NEG_BIG)
        bias = jnp.concatenate([bias] * A_GROUP, axis=0)
        new = []
        for g in range(A_KV_HEADS):
            m, l, acc = carry[g]
            q2 = q_ref[0, A_GROUP * g:A_GROUP * (g + 1)].reshape(rows2, A_HEAD_DIM)
            s = _dot_nt(q2, k_ref[0, g, pl.ds(start, tk), :]) + bias
            m_new = jnp.maximum(m, jnp.max(s, axis=-1, keepdims=True))
            p = jnp.exp2(s - m_new)
            alpha = jnp.exp2(m - m_new)
            l = alpha * l + jnp.sum(p, axis=-1, keepdims=True)
            acc = alpha * acc + _dot(p.astype(BF16), v_ref[0, g, pl.ds(start, tk), :])
            new.append((m_new, l, acc))
        return tuple(new)
    init = tuple((jnp.full((rows2, 1), NEG_BIG, F32), jnp.zeros((rows2, 1), F32), jnp.zeros((rows2, LANES), F32))
                 for _ in range(A_KV_HEADS))
    final = lax.fori_loop(0, nkb, att_body, init)
    lane = lax.broadcasted_iota(I32, (tq, LANES), 1)
    for g in range(A_KV_HEADS):
        _, l, acc = final[g]
        o = acc / l
        o_ref[0, :, g * LANES:(g + 1) * LANES] = jnp.where(lane < A_HEAD_DIM, o[0:tq], o[tq:2 * tq]).astype(o_ref.dtype)


def _dsa_prompt(q_hm, k_hm, v_dup, qi_hm, ki_bf, pg):
    b, _, s, _ = q_hm.shape
    tq = min(DSA_TQ, s)
    tk = min(DSA_TK, s)
    k_top = min(TOPK_MAX, s // 4)
    qb = lambda i, j: (i, 0, j, 0)
    whole = lambda i, j: (i, 0, 0, 0)
    kern = functools.partial(_dsa_prompt_kernel, tq=tq, tk=tk, k_top=k_top)
    return pl.pallas_call(
        kern,
        grid=(b, s // tq),
        in_specs=[
            pl.BlockSpec((1, A_HEADS, tq, A_HEAD_DIM), qb),
            pl.BlockSpec((1, A_KV_HEADS, s, A_HEAD_DIM), whole),
            pl.BlockSpec((1, A_KV_HEADS, s, LANES), whole),
            pl.BlockSpec((1, IDX_HEADS, tq, IDX_DIM), qb),
            pl.BlockSpec((1, s, IDX_DIM), lambda i, j: (i, 0, 0)),
            pl.BlockSpec((1, tq, PG_W), lambda i, j: (i, j, 0)),
        ],
        out_specs=pl.BlockSpec((1, tq, A_WIDTH), lambda i, j: (i, j, 0)),
        out_shape=jax.ShapeDtypeStruct((b, s, A_WIDTH), BF16),
        scratch_shapes=[pltpu.VMEM((s // tk, tq, tk), F32), pltpu.VMEM((tq, LANES), F32)],
        compiler_params=_cparams(("arbitrary", "arbitrary")),
        name="dsa_prompt",
    )(q_hm, k_hm, v_dup, qi_hm, ki_bf, pg)


def _dsa_sample_kernel(pt_ref, q_ref, qi_ref, w_ref, knt_ref, vnt_ref, kint_ref, *rest, g, n_pages, t_new, k_top):
    kidx_refs, kc_refs, vc_refs = rest[0:g], rest[g:2 * g], rest[2 * g:3 * g]
    o_ref, sc_ref, kres_ref, vres_ref, bias_ref = rest[3 * g:]
    p = pl.program_id(1)
    rows = SAMPLE_ROWS
    past = n_pages * PAGE_SIZE
    w = w_ref[0][:, 2 * M_HEADS:2 * M_HEADS + IDX_HEADS] * (IDX_HEADS ** -0.5)

    def scores(ki_t):
        d = jnp.maximum(_dot(qi_ref[0], ki_t), 0.0).reshape(IDX_HEADS, rows, PAGE_SIZE)
        sc = d[0] * w[:, 0:1]
        for h in range(1, IDX_HEADS):
            sc = sc + d[h] * w[:, h:h + 1]
        return sc

    for j in range(g):
        off = pl.multiple_of((p * g + j) * PAGE_SIZE, PAGE_SIZE)
        sc_ref[:, pl.ds(off, PAGE_SIZE)] = scores(kidx_refs[j][0, 0].astype(BF16))
        for h in range(A_KV_HEADS):
            kres_ref[h, :, pl.ds(off, PAGE_SIZE)] = kc_refs[j][0, 0, h].astype(BF16)
            vres_ref[h, :, pl.ds(off, PAGE_SIZE)] = vc_refs[j][0, 0, h].astype(BF16)

    @pl.when(p == pl.num_programs(1) - 1)
    def _emit():
        t_i = lax.broadcasted_iota(I32, (rows, PAGE_SIZE), 0)
        j_i = lax.broadcasted_iota(I32, (rows, PAGE_SIZE), 1)
        vis = jnp.logical_and(j_i <= t_i, j_i < t_new)
        sc_ref[:, past:past + PAGE_SIZE] = jnp.where(vis, scores(kint_ref[0]), -jnp.inf)
        for h in range(A_KV_HEADS):
            kres_ref[h, :, past:past + PAGE_SIZE] = knt_ref[0, h]
            vres_ref[h, :, past:past + PAGE_SIZE] = vnt_ref[0, h]
        _select_dense(sc_ref[...], k_top, bias_ref)
        bias = jnp.concatenate([bias_ref[...]] * A_GROUP, axis=0)
        for h in range(A_KV_HEADS):
            s = _dot(q_ref[0, h], kres_ref[h]) + bias
            pr = jnp.exp2(s - jnp.max(s, axis=-1, keepdims=True))
            o = _dot_nt(pr.astype(BF16), vres_ref[h]) / jnp.sum(pr, axis=-1, keepdims=True)
            o_ref[0, A_GROUP * h:A_GROUP * (h + 1)] = o.reshape(A_GROUP, rows, A_HEAD_DIM).astype(o_ref.dtype)


def _dsa_sample(q_g, qi_all, w8, kn_t, vn_t, kin_t, ck_t, cv_t, cidx_t, page_table, *, layer, t_new):
    db = q_g.shape[0]
    n_pages = page_table.shape[1]
    g = PAGES_PER_STEP if n_pages % PAGES_PER_STEP == 0 else n_pages
    n_steps = n_pages // g
    n_keys = (n_pages + 1) * PAGE_SIZE
    k_top = min(TOPK_MAX, (n_pages * PAGE_SIZE + t_new) // 4)
    per_b3 = lambda b, p, pt: (b, 0, 0)
    per_b4 = lambda b, p, pt: (b, 0, 0, 0)

    def page4(j):
        return lambda b, p, pt: (layer, pt[b * n_pages + p * g + j], 0, 0)

    def page5(j):
        return lambda b, p, pt: (layer, pt[b * n_pages + p * g + j], 0, 0, 0)

    in_specs = [
        pl.BlockSpec((1, A_KV_HEADS, A_GROUP * SAMPLE_ROWS, A_HEAD_DIM), per_b4),
        pl.BlockSpec((1, IDX_HEADS * SAMPLE_ROWS, IDX_DIM), per_b3),
        pl.BlockSpec((1, SAMPLE_ROWS, PG_W), per_b3),
        pl.BlockSpec((1, A_KV_HEADS, A_HEAD_DIM, PAGE_SIZE), per_b4),
        pl.BlockSpec((1, A_KV_HEADS, A_HEAD_DIM, PAGE_SIZE), per_b4),
        pl.BlockSpec((1, IDX_DIM, PAGE_SIZE), per_b3),
    ]
    in_specs += [pl.BlockSpec((1, 1, IDX_DIM, PAGE_SIZE), page4(j)) for j in range(g)]
    in_specs += [pl.BlockSpec((1, 1, A_KV_HEADS, A_HEAD_DIM, PAGE_SIZE), page5(j)) for j in range(g)]
    in_specs += [pl.BlockSpec((1, 1, A_KV_HEADS, A_HEAD_DIM, PAGE_SIZE), page5(j)) for j in range(g)]
    grid_spec = pltpu.PrefetchScalarGridSpec(
        num_scalar_prefetch=1,
        grid=(db, n_steps),
        in_specs=in_specs,
        out_specs=pl.BlockSpec((1, A_HEADS, SAMPLE_ROWS, A_HEAD_DIM), per_b4),
        scratch_shapes=[
            pltpu.VMEM((SAMPLE_ROWS, n_keys), F32),
            pltpu.VMEM((A_KV_HEADS, A_HEAD_DIM, n_keys), BF16),
            pltpu.VMEM((A_KV_HEADS, A_HEAD_DIM, n_keys), BF16),
            pltpu.VMEM((SAMPLE_ROWS, n_keys), F32),
        ],
    )
    kern = functools.partial(_dsa_sample_kernel, g=g, n_pages=n_pages, t_new=t_new, k_top=k_top)
    return pl.pallas_call(
        kern, grid_spec=grid_spec,
        out_shape=jax.ShapeDtypeStruct((db, A_HEADS, SAMPLE_ROWS, A_HEAD_DIM), BF16),
        compiler_params=_cparams(("arbitrary", "arbitrary")),
        name="dsa_sample",
    )(page_table.reshape(-1), q_g, qi_all, w8, kn_t, vn_t, kin_t, *([cidx_t] * g), *([ck_t] * g), *([cv_t] * g))


def _sample_layouts(q_hm, qi_hm, k_o, v_o, ki_o, pg):
    db, _, t, _ = q_hm.shape
    pad_t = SAMPLE_ROWS - t
    pad_rows = lambda x: jnp.pad(x, ((0, 0), (0, 0), (0, pad_t), (0, 0)))
    q_g = pad_rows(q_hm).reshape(db, A_KV_HEADS, A_GROUP * SAMPLE_ROWS, A_HEAD_DIM)
    qi_all = pad_rows(qi_hm).reshape(db, IDX_HEADS * SAMPLE_ROWS, IDX_DIM)
    w8 = jnp.pad(pg, ((0, 0), (0, pad_t), (0, 0)))

    def heads_t(x):
        xt = x.reshape(db, t, A_KV_HEADS, A_HEAD_DIM).transpose(0, 2, 3, 1)
        return jnp.pad(xt, ((0, 0), (0, 0), (0, 0), (0, PAGE_SIZE - t))).astype(BF16)

    kin_t = jnp.pad(ki_o.transpose(0, 2, 1), ((0, 0), (0, 0), (0, PAGE_SIZE - t))).astype(BF16)
    return q_g, qi_all, w8, heads_t(k_o), heads_t(v_o), kin_t


def _finish_kernel(x_ref, hm_ref, a_ref, wo_ref, gf_ref, wgu_ref, wd_ref, y_ref):
    x1 = (x_ref[...] + _dot(hm_ref[...], wo_ref[0:M_WIDTH, :]) + _dot(a_ref[...], wo_ref[M_WIDTH:M_WIDTH + A_WIDTH, :]))
    ms = jnp.mean(x1 * x1, axis=-1, keepdims=True)
    xn = (x1 * lax.rsqrt(ms + NORM_EPS) * gf_ref[...]).astype(BF16)
    g = _dot(xn, wgu_ref[:, 0:D_FF])
    u = _dot(xn, wgu_ref[:, D_FF:2 * D_FF])
    y_ref[...] = x1 + _dot((g * jax.nn.sigmoid(g) * u).astype(BF16), wd_ref[...])


def _finish(x, hm, a, w_out, g_ffn, w_gu, w_down):
    n = x.shape[0]
    tm = min(ROW_TILE, n)
    const = lambda i: (0, 0)
    row = lambda i: (i, 0)
    return pl.pallas_call(
        _finish_kernel,
        grid=(n // tm,),
        in_specs=[
            pl.BlockSpec((tm, D_MODEL), row),
            pl.BlockSpec((tm, M_WIDTH), row),
            pl.BlockSpec((tm, A_WIDTH), row),
            pl.BlockSpec((M_WIDTH + A_WIDTH, D_MODEL), const),
            pl.BlockSpec((1, D_MODEL), const),
            pl.BlockSpec((D_MODEL, 2 * D_FF), const),
            pl.BlockSpec((D_FF, D_MODEL), const),
        ],
        out_specs=pl.BlockSpec((tm, D_MODEL), row),
        out_shape=jax.ShapeDtypeStruct((n, D_MODEL), F32),
        compiler_params=_cparams(("arbitrary",)),
        name="finish",
    )(x, hm, a, w_out, g_ffn, w_gu, w_down)


def _rope_tables(pos):
    half = A_HEAD_DIM // 2
    inv = ROPE_THETA ** (-jnp.arange(half, dtype=F32) / half)
    ang = pos.astype(F32)[:, None] * inv[None, :]
    cos, sin = jnp.cos(ang), jnp.sin(ang)
    cos_t = jnp.tile(jnp.concatenate([cos, cos], axis=-1), (1, LANES // A_HEAD_DIM))
    sin_t = jnp.tile(jnp.concatenate([-sin, sin], axis=-1), (1, LANES // A_HEAD_DIM))
    return cos_t, sin_t


def _layer_weights(l, g_mix, w_in, conv_w, b_gate, g_mout, g_q, g_k, w_out, g_ffn, w_gate_up, w_down):
    w = w_in[l]
    o_mi = PM_W
    o_aq = o_mi + 2 * M_HEADS
    o_iw = o_aq + PA_W + IDX_HEADS * IDX_DIM + IDX_DIM
    w_main = jnp.concatenate(
        [w[:, 0:PM_W], w[:, o_aq:o_iw], jnp.zeros((D_MODEL, LANES - IDX_DIM), F32)], axis=1).astype(BF16)
    w_gate = jnp.concatenate(
        [w[:, o_mi:o_aq], w[:, o_iw:o_iw + IDX_HEADS], jnp.zeros((D_MODEL, PG_W - 2 * M_HEADS - IDX_HEADS), F32)], axis=1)
    wg_hi, wg_lo = _split2(w_gate)
    bias_row = jnp.concatenate([b_gate[l], jnp.zeros((PG_W - 2 * M_HEADS,), F32)])[None, :]
    tile2 = lambda g: jnp.tile(g, LANES // A_HEAD_DIM)[None, :]
    return dict(
        g_mix=g_mix[l][None, :], w_main=w_main, wg_hi=wg_hi, wg_lo=wg_lo,
        conv_w=conv_w[l], bias_row=bias_row, g_mout=g_mout[l][None, :],
        gq=tile2(g_q[l]), gk=tile2(g_k[l]),
        w_out=w_out[l].astype(BF16), g_ffn=g_ffn[l][None, :],
        w_gu=w_gate_up[l].astype(BF16), w_down=w_down[l].astype(BF16))


def kernel(x_prompt, x_sample, cache_k, cache_v, cache_kidx, page_table, state_C, state_n, state_m, state_conv,
           g_mix, w_in, conv_w, b_gate, g_mout, g_q, g_k, w_out, g_ffn, w_gate_up, w_down):
    b, s, _ = x_prompt.shape
    db, t, _ = x_sample.shape
    depth = w_in.shape[0]
    past = page_table.shape[1] * PAGE_SIZE
    cos_p, sin_p = _rope_tables(jnp.arange(s, dtype=I32))
    cos_s, sin_s = _rope_tables(past + jnp.arange(t, dtype=I32))
    seg = (jnp.arange(LANES)[:, None] // A_HEAD_DIM == jnp.arange(LANES)[None, :] // A_HEAD_DIM).astype(BF16)
    ck_t = cache_k.transpose(0, 1, 3, 4, 2)
    cv_t = cache_v.transpose(0, 1, 3, 4, 2)
    cidx_t = cache_kidx.transpose(0, 1, 3, 2)
    xp = x_prompt.reshape(b * s, D_MODEL)
    xs = x_sample.reshape(db * t, D_MODEL)
    kp, vp, kip, cp, np_, mp, bp = [], [], [], [], [], [], []
    ks_, vs_, kis, cs, ns, ms, bs = [], [], [], [], [], [], []
    for l in range(depth):
        wl = _layer_weights(l, g_mix, w_in, conv_w, b_gate, g_mout, g_q, g_k, w_out, g_ffn, w_gate_up, w_down)
        pm, pa, pi, pg = _project(xp, wl["g_mix"], wl["w_main"], wl["wg_hi"], wl["wg_lo"])
        pm3, pg3 = pm.reshape(b, s, PM_W), pg.reshape(b, s, PG_W)
        hm, c_o, n_o, m_o, conv_o = _mlstm(pm3, pg3, wl["conv_w"], wl["bias_row"], wl["g_mout"], None)
        q_hm, k_hm, v_dup, qi_hm, ki_bf, k_o, v_o, ki_o = _aprep(
            pa.reshape(b, s, PA_W), pi.reshape(b, s, PI_W), cos_p, sin_p, wl["gq"], wl["gk"], seg)
        a = _dsa_prompt(q_hm, k_hm, v_dup, qi_hm, ki_bf, pg3)
        xp = _finish(xp, hm.reshape(b * s, M_WIDTH), a.reshape(b * s, A_WIDTH),
                     wl["w_out"], wl["g_ffn"], wl["w_gu"], wl["w_down"])
        kp.append(k_o.reshape(b, s, A_KV_HEADS, A_HEAD_DIM))
        vp.append(v_o.reshape(b, s, A_KV_HEADS, A_HEAD_DIM))
        kip.append(ki_o)
        cp.append(c_o)
        np_.append(n_o[:, :, 0, :])
        mp.append(m_o[:, :, 0, 0])
        bp.append(conv_o)

        pm, pa, pi, pg = _project(xs, wl["g_mix"], wl["w_main"], wl["wg_hi"], wl["wg_lo"])
        pg3 = pg.reshape(db, t, PG_W)
        state = (state_C[l], state_n[l][:, :, None, :],
                 jnp.broadcast_to(state_m[l][:, :, None, None], (db, M_HEADS, 1, LANES)), state_conv[l])
        hm, c_o, n_o, m_o, conv_o = _mlstm(pm.reshape(db, t, PM_W), pg3, wl["conv_w"], wl["bias_row"],
                                           wl["g_mout"], state)
        q_hm, _, _, qi_hm, _, k_o, v_o, ki_o = _aprep(
            pa.reshape(db, t, PA_W), pi.reshape(db, t, PI_W), cos_s, sin_s, wl["gq"], wl["gk"], seg)
        a = _dsa_sample(*_sample_layouts(q_hm, qi_hm, k_o, v_o, ki_o, pg3),
                        ck_t, cv_t, cidx_t, page_table, layer=l, t_new=t)
        a = a[:, :, 0:t, :].transpose(0, 2, 1, 3).reshape(db * t, A_WIDTH)
        xs = _finish(xs, hm.reshape(db * t, M_WIDTH), a, wl["w_out"], wl["g_ffn"], wl["w_gu"], wl["w_down"])
        ks_.append(k_o.reshape(db, t, A_KV_HEADS, A_HEAD_DIM))
        vs_.append(v_o.reshape(db, t, A_KV_HEADS, A_HEAD_DIM))
        kis.append(ki_o)
        cs.append(c_o)
        ns.append(n_o[:, :, 0, :])
        ms.append(m_o[:, :, 0, 0])
        bs.append(conv_o)
    return (xp.reshape(b, s, D_MODEL), xs.reshape(db, t, D_MODEL),
            jnp.stack(kp), jnp.stack(vp), jnp.stack(kip), jnp.stack(cp), jnp.stack(np_), jnp.stack(mp), jnp.stack(bp),
            jnp.stack(ks_), jnp.stack(vs_), jnp.stack(kis), jnp.stack(cs), jnp.stack(ns), jnp.stack(ms), jnp.stack(bs))
```

```python
import functools
import math

import jax
import jax.numpy as jnp
from jax import lax
from jax.experimental import pallas as pl
from jax.experimental.pallas import tpu as pltpu

F32 = jnp.float32
BF16 = jnp.bfloat16
I32 = jnp.int32

D_MODEL = 1024
M_HEADS = 4
M_HEAD_DIM = 128
M_WIDTH = M_HEADS * M_HEAD_DIM
CONV_WIDTH = 4
A_HEAD_DIM = 64
A_HEADS = 8
A_KV_HEADS = 4
A_GROUP = A_HEADS // A_KV_HEADS
A_WIDTH = A_HEADS * A_HEAD_DIM
A_KV_WIDTH = A_KV_HEADS * A_HEAD_DIM
IDX_HEADS = 8
IDX_DIM = 64
TOPK_MAX = 256
PAGE_SIZE = 128
ROPE_THETA = 10000.0
D_FF = 2816
NORM_EPS = 1e-6

LANES = 128
SUBLANES = 8
INT_MIN = -(2 ** 31)
NEG_BIG = -1e30
F32_LOWEST = -3.0e38
LOG2E = math.log2(math.e)

PM_W = 4 * M_WIDTH
PA_W = A_WIDTH + 2 * A_KV_WIDTH
PI_W = IDX_HEADS * IDX_DIM + LANES
PG_W = LANES
MAIN_W = PM_W + PA_W + PI_W

ROW_TILE = 256
MLSTM_CHUNK = 256
DSA_TQ = 256
DSA_TK = 512
ATT_HEADS_PER_LOOP = 2
SAMPLE_ROWS = SUBLANES
PAGES_PER_STEP = 16
N_BISECT = 16
MAX_WALK = 8
VMEM_LIMIT = 56 * 1024 * 1024


def _cparams(sem):
    return pltpu.CompilerParams(dimension_semantics=sem, vmem_limit_bytes=VMEM_LIMIT)


def _split2(x):
    hi = x.astype(BF16)
    lo = (x - hi.astype(F32)).astype(BF16)
    return hi, lo


def _split3(x):
    a = x.astype(BF16)
    r = x - a.astype(F32)
    b = r.astype(BF16)
    c = (r - b.astype(F32)).astype(BF16)
    return a, b, c


def _dot(a, b):
    return jnp.dot(a, b, preferred_element_type=F32)


def _dot_nt(a, b):
    return lax.dot_general(a, b, (((1,), (1,)), ((), ())), preferred_element_type=F32)


def _proj_kernel(x_ref, g_ref, w_ref, wgh_ref, wgl_ref, pm_ref, pa_ref, pi_ref, pg_ref):
    x = x_ref[...]
    ms = jnp.mean(x * x, axis=-1, keepdims=True)
    y = x * lax.rsqrt(ms + NORM_EPS) * g_ref[...]
    yh, yl = _split2(y)
    pm_ref[...] = _dot(yh, w_ref[:, 0:PM_W])
    pa_ref[...] = _dot(yh, w_ref[:, PM_W:PM_W + PA_W])
    pi_ref[...] = _dot(yh, w_ref[:, PM_W + PA_W:MAIN_W])
    pg_ref[...] = _dot(yh, wgh_ref[...]) + _dot(yh, wgl_ref[...]) + _dot(yl, wgh_ref[...])


def _project(x, g, w_main, wg_hi, wg_lo):
    n = x.shape[0]
    tm = min(ROW_TILE, n)
    const = lambda i: (0, 0)
    return pl.pallas_call(
        _proj_kernel,
        grid=(n // tm,),
        in_specs=[
            pl.BlockSpec((tm, D_MODEL), lambda i: (i, 0)),
            pl.BlockSpec((1, D_MODEL), const),
            pl.BlockSpec((D_MODEL, MAIN_W), const),
            pl.BlockSpec((D_MODEL, PG_W), const),
            pl.BlockSpec((D_MODEL, PG_W), const),
        ],
        out_specs=[
            pl.BlockSpec((tm, PM_W), lambda i: (i, 0)),
            pl.BlockSpec((tm, PA_W), lambda i: (i, 0)),
            pl.BlockSpec((tm, PI_W), lambda i: (i, 0)),
            pl.BlockSpec((tm, PG_W), lambda i: (i, 0)),
        ],
        out_shape=[
            jax.ShapeDtypeStruct((n, PM_W), F32),
            jax.ShapeDtypeStruct((n, PA_W), F32),
            jax.ShapeDtypeStruct((n, PI_W), F32),
            jax.ShapeDtypeStruct((n, PG_W), F32),
        ],
        compiler_params=_cparams(("arbitrary",)),
        name="proj",
    )(x, g, w_main, wg_hi, wg_lo)


def _log_sigmoid(x):
    return jnp.minimum(x, 0.0) - jnp.log1p(jnp.exp(-jnp.abs(x)))


def _mlstm_kernel(*refs, lc, tb, tv, nc, has_state):
    if has_state:
        (pm_ref, pg_ref, cw_ref, bias_ref, gm_ref, c0_ref, n0_ref, m0_ref, conv0_ref,
         h_ref, c_out_ref, n_out_ref, m_out_ref, conv_out_ref,
         c_scr, n_scr, m_scr, ext_scr, u_scr) = refs
    else:
        (pm_ref, pg_ref, cw_ref, bias_ref, gm_ref,
         h_ref, c_out_ref, n_out_ref, m_out_ref, conv_out_ref,
         c_scr, n_scr, m_scr, ext_scr, u_scr) = refs
    c = pl.program_id(1)
    qk_w = 2 * M_WIDTH

    @pl.when(c == 0)
    def _init():
        ext_scr[0:SUBLANES, :] = jnp.zeros((SUBLANES, qk_w), F32)
        if has_state:
            c_scr[...] = c0_ref[0]
            n_scr[...] = n0_ref[0]
            m_scr[...] = m0_ref[0]
            ext_scr[SUBLANES - (CONV_WIDTH - 1):SUBLANES, :] = conv0_ref[0]
        else:
            c_scr[...] = jnp.zeros(c_scr.shape, F32)
            n_scr[...] = jnp.zeros(n_scr.shape, F32)
            m_scr[...] = jnp.zeros(m_scr.shape, F32)

    if tb < lc:
        u_scr[...] = jnp.zeros(u_scr.shape, F32)
        u_scr[0:tb, 0:PM_W] = pm_ref[0]
        u_scr[0:tb, PM_W:PM_W + PG_W] = pg_ref[0]
        pm = u_scr[:, 0:PM_W]
        gates = u_scr[:, PM_W:PM_W + PG_W]
    else:
        pm = pm_ref[0]
        gates = pg_ref[0]

    ext_scr[SUBLANES:SUBLANES + lc, :] = pm[:, 0:qk_w]
    cw = cw_ref[...]
    qk = ext_scr[pl.ds(SUBLANES - 3, lc), :] * cw[0:1, :]
    for j in range(1, CONV_WIDTH):
        qk = qk + ext_scr[pl.ds(SUBLANES - 3 + j, lc), :] * cw[j:j + 1, :]
    new_tail = ext_scr[pl.ds(tv, SUBLANES), :]
    qk = qk * jax.nn.sigmoid(qk)

    a = gates + bias_ref[...]
    ig = a
    lf = _log_sigmoid(a)
    if tv < lc:
        valid = lax.broadcasted_iota(I32, (lc, LANES), 0) < tv
        ig = jnp.where(valid, ig, NEG_BIG)
        lf = jnp.where(valid, lf, 0.0)
    ig_t = ig.T[0:SUBLANES, :]
    lf_t = lf.T[0:SUBLANES, :]

    r_i = lax.broadcasted_iota(I32, (lc, lc), 0)
    c_i = lax.broadcasted_iota(I32, (lc, lc), 1)
    causal = c_i <= r_i
    tril = jnp.where(causal, 1.0, 0.0).astype(BF16)
    triu = jnp.where(r_i <= c_i, 1.0, 0.0).astype(BF16)
    b_cols = sum(_dot(tril, p) for p in _split3(lf))
    b_rows = sum(_dot(p, triu) for p in _split3(lf_t))

    outs = []
    for h in range(M_HEADS):
        lo, hi = h * M_HEAD_DIM, (h + 1) * M_HEAD_DIM
        q = qk[:, lo:hi]
        k = qk[:, M_WIDTH + lo:M_WIDTH + hi] * (M_HEAD_DIM ** -0.5)
        v = pm[:, 2 * M_WIDTH + lo:2 * M_WIDTH + hi]
        og = pm[:, 3 * M_WIDTH + lo:3 * M_WIDTH + hi]
        b_col = b_cols[:, M_HEADS + h:M_HEADS + h + 1]
        i_col = ig[:, h:h + 1]
        b_row = b_rows[M_HEADS + h:M_HEADS + h + 1, :]
        i_row = ig_t[h:h + 1, :]
        m_prev = m_scr[h][:, 0:1]
        c_h = c_scr[h]
        n_h = n_scr[h]

        log_d = jnp.where(causal, b_col - b_row + i_row, NEG_BIG)
        inter = b_col + m_prev
        m_t = jnp.maximum(jnp.max(log_d, axis=-1, keepdims=True), inter)
        w_inter = jnp.exp(inter - m_t)
        d_mat = jnp.exp(log_d - m_t)
        qb = q.astype(BF16)
        s = _dot_nt(qb, k.astype(BF16)) * d_mat
        num = w_inter * _dot_nt(qb, c_h.astype(BF16)) + _dot(s.astype(BF16), v.astype(BF16))
        nq = w_inter * jnp.sum(q * n_h, axis=-1, keepdims=True) + jnp.sum(s, axis=-1, keepdims=True)
        hh = num / jnp.maximum(jnp.abs(nq), jnp.exp(-m_t))
        ms = jnp.mean(hh * hh, axis=-1, keepdims=True)
        hn = hh * lax.rsqrt(ms + NORM_EPS) * gm_ref[:, lo:hi]
        outs.append(jax.nn.sigmoid(og) * hn)

        b_last = b_cols[lc - 1:lc, M_HEADS + h:M_HEADS + h + 1]
        log_w = b_last - b_col + i_col
        m_new = jnp.maximum(b_last + m_prev, jnp.max(log_w, axis=0, keepdims=True))
        decay = jnp.exp(b_last + m_prev - m_new)
        kw = k * jnp.exp(log_w - m_new)
        c_scr[h] = decay * c_h + _dot(v.T.astype(BF16), kw.astype(BF16))
        n_scr[h] = decay * n_h + jnp.sum(kw, axis=0, keepdims=True)
        m_scr[h] = jnp.broadcast_to(m_new, (1, LANES))

    out = jnp.concatenate(outs, axis=-1)
    h_ref[0] = out[0:tb].astype(h_ref.dtype)
    ext_scr[0:SUBLANES, :] = new_tail

    @pl.when(c == nc - 1)
    def _fin():
        c_out_ref[0] = c_scr[...]
        n_out_ref[0] = n_scr[...]
        m_out_ref[0] = m_scr[...]
        conv_out_ref[0] = new_tail[SUBLANES - (CONV_WIDTH - 1):SUBLANES, :]


def _mlstm(pm, pg, conv_w, bias_row, g_mout, state):
    b, t, _ = pm.shape
    if t >= MLSTM_CHUNK:
        lc, tb, tv = MLSTM_CHUNK, MLSTM_CHUNK, MLSTM_CHUNK
    else:
        lc, tb, tv = LANES, t, t
    nc = max(t // lc, 1)
    has_state = state is not None
    qk_w = 2 * M_WIDTH
    const2 = lambda i, j: (0, 0)
    per_b4 = lambda i, j: (i, 0, 0, 0)
    in_specs = [
        pl.BlockSpec((1, tb, PM_W), lambda i, j: (i, j, 0)),
        pl.BlockSpec((1, tb, PG_W), lambda i, j: (i, j, 0)),
        pl.BlockSpec((CONV_WIDTH, qk_w), const2),
        pl.BlockSpec((1, PG_W), const2),
        pl.BlockSpec((1, M_WIDTH), const2),
    ]
    args = [pm, pg, conv_w, bias_row, g_mout]
    if has_state:
        in_specs += [
            pl.BlockSpec((1, M_HEADS, M_HEAD_DIM, M_HEAD_DIM), per_b4),
            pl.BlockSpec((1, M_HEADS, 1, M_HEAD_DIM), per_b4),
            pl.BlockSpec((1, M_HEADS, 1, LANES), per_b4),
            pl.BlockSpec((1, CONV_WIDTH - 1, qk_w), lambda i, j: (i, 0, 0)),
        ]
        args += list(state)
    out_specs = [
        pl.BlockSpec((1, tb, M_WIDTH), lambda i, j: (i, j, 0)),
        pl.BlockSpec((1, M_HEADS, M_HEAD_DIM, M_HEAD_DIM), per_b4),
        pl.BlockSpec((1, M_HEADS, 1, M_HEAD_DIM), per_b4),
        pl.BlockSpec((1, M_HEADS, 1, LANES), per_b4),
        pl.BlockSpec((1, CONV_WIDTH - 1, qk_w), lambda i, j: (i, 0, 0)),
    ]
    out_shape = [
        jax.ShapeDtypeStruct((b, t, M_WIDTH), BF16),
        jax.ShapeDtypeStruct((b, M_HEADS, M_HEAD_DIM, M_HEAD_DIM), F32),
        jax.ShapeDtypeStruct((b, M_HEADS, 1, M_HEAD_DIM), F32),
        jax.ShapeDtypeStruct((b, M_HEADS, 1, LANES), F32),
        jax.ShapeDtypeStruct((b, CONV_WIDTH - 1, qk_w), F32),
    ]
    scratch = [
        pltpu.VMEM((M_HEADS, M_HEAD_DIM, M_HEAD_DIM), F32),
        pltpu.VMEM((M_HEADS, 1, M_HEAD_DIM), F32),
        pltpu.VMEM((M_HEADS, 1, LANES), F32),
        pltpu.VMEM((lc + 2 * SUBLANES, qk_w), F32),
        pltpu.VMEM((lc, PM_W + PG_W), F32),
    ]
    kern = functools.partial(_mlstm_kernel, lc=lc, tb=tb, tv=tv, nc=nc, has_state=has_state)
    return pl.pallas_call(
        kern, grid=(b, nc), in_specs=in_specs, out_specs=out_specs, out_shape=out_shape,
        scratch_shapes=scratch, compiler_params=_cparams(("arbitrary", "arbitrary")), name="mlstm",
    )(*args)


def _swap_halves(x):
    lane = lax.broadcasted_iota(I32, x.shape, 1)
    first = (lane % A_HEAD_DIM) < (A_HEAD_DIM // 2)
    return jnp.where(first, pltpu.roll(x, LANES - A_HEAD_DIM // 2, 1), pltpu.roll(x, A_HEAD_DIM // 2, 1))


def _rope(x, cos, sin):
    return x * cos + _swap_halves(x) * sin


def _head_rms(x, seg, g):
    hi, lo = _split2(x * x)
    ss = _dot(hi, seg) + _dot(lo, seg)
    return x * lax.rsqrt(ss * (1.0 / A_HEAD_DIM) + NORM_EPS) * g


def _aprep_kernel(pa_ref, pi_ref, cos_ref, sin_ref, gq_ref, gk_ref, seg_ref,
                  q_ref, kh_ref, vd_ref, qi_ref, kib_ref, k_out_ref, v_out_ref, ki_out_ref):
    cos = cos_ref[...]
    sin = sin_ref[...]
    seg = seg_ref[...]
    lane = lax.broadcasted_iota(I32, cos.shape, 1)
    low = lane < A_HEAD_DIM
    half = A_HEAD_DIM

    def split_heads(blk):
        return blk[:, 0:half], pltpu.roll(blk, half, 1)[:, 0:half]

    for j in range(A_WIDTH // LANES):
        blk = _rope(_head_rms(pa_ref[0, :, j * LANES:(j + 1) * LANES], seg, gq_ref[...]), cos, sin)
        blk = blk * (A_HEAD_DIM ** -0.5 * LOG2E)
        h0, h1 = split_heads(blk)
        q_ref[0, 2 * j] = h0.astype(BF16)
        q_ref[0, 2 * j + 1] = h1.astype(BF16)
    for j in range(A_KV_WIDTH // LANES):
        blk = _rope(_head_rms(pa_ref[0, :, A_WIDTH + j * LANES:A_WIDTH + (j + 1) * LANES], seg, gk_ref[...]), cos, sin)
        k_out_ref[0, :, j * LANES:(j + 1) * LANES] = blk
        h0, h1 = split_heads(blk)
        kh_ref[0, 2 * j] = h0.astype(BF16)
        kh_ref[0, 2 * j + 1] = h1.astype(BF16)
        vb = pa_ref[0, :, A_WIDTH + A_KV_WIDTH + j * LANES:A_WIDTH + A_KV_WIDTH + (j + 1) * LANES]
        v_out_ref[0, :, j * LANES:(j + 1) * LANES] = vb
        vr = pltpu.roll(vb, half, 1)
        vd_ref[0, 2 * j] = jnp.where(low, vb, vr).astype(BF16)
        vd_ref[0, 2 * j + 1] = jnp.where(low, vr, vb).astype(BF16)
    for j in range(IDX_HEADS * IDX_DIM // LANES):
        blk = _rope(pi_ref[0, :, j * LANES:(j + 1) * LANES], cos, sin) * (IDX_DIM ** -0.5)
        h0, h1 = split_heads(blk)
        qi_ref[0, 2 * j] = h0.astype(BF16)
        qi_ref[0, 2 * j + 1] = h1.astype(BF16)
    kblk = _rope(pi_ref[0, :, IDX_HEADS * IDX_DIM:IDX_HEADS * IDX_DIM + LANES], cos, sin)[:, 0:half]
    ki_out_ref[0] = kblk
    kib_ref[0] = kblk.astype(BF16)


def _aprep(pa, pi, cos, sin, gq, gk, seg):
    b, t, _ = pa.shape
    tm = min(ROW_TILE, t)
    c2 = lambda i, j: (0, 0)
    row3 = lambda i, j: (i, j, 0)
    hm = lambda i, j: (i, 0, j, 0)
    return pl.pallas_call(
        _aprep_kernel,
        grid=(b, t // tm),
        in_specs=[
            pl.BlockSpec((1, tm, PA_W), row3),
            pl.BlockSpec((1, tm, PI_W), row3),
            pl.BlockSpec((tm, LANES), lambda i, j: (j, 0)),
            pl.BlockSpec((tm, LANES), lambda i, j: (j, 0)),
            pl.BlockSpec((1, LANES), c2),
            pl.BlockSpec((1, LANES), c2),
            pl.BlockSpec((LANES, LANES), c2),
        ],
        out_specs=[
            pl.BlockSpec((1, A_HEADS, tm, A_HEAD_DIM), hm),
            pl.BlockSpec((1, A_KV_HEADS, tm, A_HEAD_DIM), hm),
            pl.BlockSpec((1, A_KV_HEADS, tm, LANES), hm),
            pl.BlockSpec((1, IDX_HEADS, tm, IDX_DIM), hm),
            pl.BlockSpec((1, tm, IDX_DIM), row3),
            pl.BlockSpec((1, tm, A_KV_WIDTH), row3),
            pl.BlockSpec((1, tm, A_KV_WIDTH), row3),
            pl.BlockSpec((1, tm, IDX_DIM), row3),
        ],
        out_shape=[
            jax.ShapeDtypeStruct((b, A_HEADS, t, A_HEAD_DIM), BF16),
            jax.ShapeDtypeStruct((b, A_KV_HEADS, t, A_HEAD_DIM), BF16),
            jax.ShapeDtypeStruct((b, A_KV_HEADS, t, LANES), BF16),
            jax.ShapeDtypeStruct((b, IDX_HEADS, t, IDX_DIM), BF16),
            jax.ShapeDtypeStruct((b, t, IDX_DIM), BF16),
            jax.ShapeDtypeStruct((b, t, A_KV_WIDTH), F32),
            jax.ShapeDtypeStruct((b, t, A_KV_WIDTH), F32),
            jax.ShapeDtypeStruct((b, t, IDX_DIM), F32),
        ],
        compiler_params=_cparams(("arbitrary", "arbitrary")),
        name="aprep",
    )(pa, pi, cos, sin, gq, gk, seg)


def _sortable_key(score):
    bits = lax.bitcast_convert_type(score, I32)
    key = bits ^ ((bits >> 31) & 0x7FFFFFFF)
    key = jnp.where(key == -1, 0, key)
    return jnp.where(score == -jnp.inf, INT_MIN, key)


def _fold_lanes(x):
    acc = x[:, 0:LANES]
    for j in range(1, x.shape[1] // LANES):
        acc = acc + x[:, j * LANES:(j + 1) * LANES]
    return acc


def _fold_lanes_min(x):
    acc = x[:, 0:LANES]
    for j in range(1, x.shape[1] // LANES):
        acc = jnp.minimum(acc, x[:, j * LANES:(j + 1) * LANES])
    return acc


def _row_count(mask):
    return jnp.sum(_fold_lanes(jnp.where(mask, 1.0, 0.0)), axis=-1, keepdims=True)


def _bisect(count_ge, lo, hi, k_top):
    def body(_, c):
        lo, hi = c
        mid = lo + (hi - lo) * 0.5
        ge = count_ge(mid) >= k_top
        return jnp.where(ge, mid, lo), jnp.where(ge, hi, mid)
    return lax.fori_loop(0, N_BISECT, body, (lo, hi))[0]


def _selected(sc, thr, strict):
    at_least = jnp.where(strict > 0.0, F32_LOWEST, thr)
    above = jnp.where(strict > 0.0, thr, F32_LOWEST)
    return jnp.logical_and(sc >= at_least, sc > above)


def _walk_up(lo, c_lo, k_top, min_selected, count_gt):
    def cond(c):
        return jnp.logical_and(c[5], c[6] < MAX_WALK)

    def body(c):
        thr, strict, c_sel, c_gt, tie, _, it = c
        active = jnp.logical_and(c_sel > k_top, tie == 0.0)
        v = min_selected(thr, strict)
        c_above = count_gt(v)
        adv = jnp.logical_and(active, c_above >= k_top)
        stop = jnp.logical_and(active, c_above < k_top)
        again = jnp.max(jnp.where(jnp.logical_and(adv, c_above > k_top), 1.0, 0.0)) > 0.0
        return (jnp.where(active, v, thr), jnp.where(adv, 1.0, jnp.where(stop, 0.0, strict)),
                jnp.where(adv, c_above, c_sel), jnp.where(stop, c_above, c_gt), jnp.where(stop, 1.0, tie),
                again, it + 1)

    zero = jnp.zeros_like(lo)
    thr, strict, _, c_gt, tie, pending, _ = lax.while_loop(
        cond, body, (lo, zero, c_lo, zero, zero, jnp.max(c_lo) > k_top, jnp.int32(0)))
    return thr, strict, c_gt, tie, pending


def _kth_key(count_ge_key, rows, k_top):
    def body(it, thr):
        cand = thr ^ lax.shift_left(jnp.int32(1), 31 - it)
        return jnp.where(count_ge_key(cand) >= k_top, cand, thr)
    return lax.fori_loop(0, 32, body, jnp.full((rows, 1), INT_MIN, I32))


def _tie_rank(eq, before):
    n = eq.shape[1]
    r_i = lax.broadcasted_iota(I32, (n, n), 0)
    c_i = lax.broadcasted_iota(I32, (n, n), 1)
    triu = jnp.where(r_i <= c_i, 1.0, 0.0).astype(BF16)
    return before + _dot(jnp.where(eq, 1.0, 0.0).astype(BF16), triu)


def _dense_tie_ranks(eq):
    before = jnp.zeros((eq.shape[0], 1), F32)
    ranks = []
    for j in range(eq.shape[1] // LANES):
        eq_j = eq[:, j * LANES:(j + 1) * LANES]
        ranks.append(_tie_rank(eq_j, before))
        before = before + _row_count(eq_j)
    return jnp.concatenate(ranks, axis=-1)


def _select_dense(sc, k_top, bias_ref):
    rows, n = sc.shape
    lo0 = jnp.min(jnp.where(sc == -jnp.inf, jnp.inf, sc), axis=-1, keepdims=True)
    hi0 = jnp.max(sc, axis=-1, keepdims=True)
    count_ge = lambda thr: _row_count(sc >= thr)
    lo = _bisect(count_ge, lo0, hi0, k_top)
    c_lo = count_ge(lo)
    bias_ref[...] = jnp.where(sc >= lo, 0.0, NEG_BIG)

    @pl.when(jnp.max(c_lo) > k_top)
    def _refine():
        min_selected = lambda thr, strict: jnp.min(
            _fold_lanes_min(jnp.where(_selected(sc, thr, strict), sc, jnp.inf)), axis=-1, keepdims=True)
        thr, strict, c_gt, tie, unresolved = _walk_up(lo, c_lo, k_top, min_selected, lambda v: _row_count(sc > v))

        @pl.when(jnp.logical_not(unresolved))
        def _ties():
            eq = jnp.logical_and(sc == thr, tie > 0.0)
            drop = jnp.logical_and(eq, _dense_tie_ranks(eq) > k_top - c_gt)
            keep = jnp.logical_and(_selected(sc, thr, strict), jnp.logical_not(drop))
            bias_ref[...] = jnp.where(keep, 0.0, NEG_BIG)

        @pl.when(unresolved)
        def _exact():
            key = _sortable_key(sc)
            tk_ = _kth_key(lambda c: _row_count(key >= c), rows, k_top)
            eq = key == tk_
            need = k_top - (_row_count(key >= tk_) - _row_count(eq))
            keep_eq = jnp.logical_and(eq, jnp.logical_and(_dense_tie_ranks(eq) <= need, tk_ > INT_MIN))
            bias_ref[...] = jnp.where(jnp.logical_or(key > tk_, keep_eq), 0.0, NEG_BIG)


def _dsa_prompt_kernel(q_ref, k_ref, v_ref, qi_ref, ki_ref, pg_ref, o_ref, sc_ref, thr_ref, strict_ref,
                       *, tq, tk, k_top):
    i = pl.program_id(1)
    r0 = i * tq
    nkb = (r0 + tq + tk - 1) // tk
    row = r0 + lax.broadcasted_iota(I32, (tq, tk), 0)
    col0 = lax.broadcasted_iota(I32, (tq, tk), 1)

    qi = qi_ref[0].reshape(IDX_HEADS * tq, IDX_DIM)
    w = pg_ref[0][:, 2 * M_HEADS:2 * M_HEADS + IDX_HEADS] * (IDX_HEADS ** -0.5)

    def score_body(kb, carry):
        lo, hi = carry
        start = pl.multiple_of(kb * tk, tk)
        d = _dot_nt(qi, ki_ref[0, pl.ds(start, tk), :])
        d = jnp.maximum(d, 0.0).reshape(IDX_HEADS, tq, tk)
        sc = d[0] * w[:, 0:1]
        for h in range(1, IDX_HEADS):
            sc = sc + d[h] * w[:, h:h + 1]
        vis = col0 + kb * tk <= row
        sc_ref[kb] = jnp.where(vis, sc, -jnp.inf)
        lo = jnp.minimum(lo, jnp.min(jnp.where(vis, sc, jnp.inf), axis=-1, keepdims=True))
        hi = jnp.maximum(hi, jnp.max(jnp.where(vis, sc, -jnp.inf), axis=-1, keepdims=True))
        return lo, hi
    lo0, hi0 = lax.fori_loop(0, nkb, score_body,
                             (jnp.full((tq, 1), jnp.inf, F32), jnp.full((tq, 1), -jnp.inf, F32)))

    def count_blocks(pred):
        def body(kb, acc):
            return acc + _fold_lanes(jnp.where(pred(sc_ref[kb]), 1.0, 0.0))
        acc = lax.fori_loop(0, nkb, body, jnp.zeros((tq, LANES), F32))
        return jnp.sum(acc, axis=-1, keepdims=True)

    def min_selected(thr, strict):
        def body(kb, acc):
            sc = sc_ref[kb]
            return jnp.minimum(acc, _fold_lanes_min(jnp.where(_selected(sc, thr, strict), sc, jnp.inf)))
        acc = lax.fori_loop(0, nkb, body, jnp.full((tq, LANES), jnp.inf, F32))
        return jnp.min(acc, axis=-1, keepdims=True)

    count_ge = lambda thr: count_blocks(lambda sc: sc >= thr)
    lo = _bisect(count_ge, lo0, hi0, k_top)
    c_lo = count_ge(lo)
    thr_ref[...] = jnp.broadcast_to(lo, (tq, LANES))
    strict_ref[...] = jnp.zeros((tq, LANES), F32)

    @pl.when(jnp.max(c_lo) > k_top)
    def _refine():
        thr, strict, c_gt, tie, unresolved = _walk_up(
            lo, c_lo, k_top, min_selected, lambda v: count_blocks(lambda sc: sc > v))
        thr_ref[...] = jnp.broadcast_to(thr, (tq, LANES))
        strict_ref[...] = jnp.broadcast_to(strict, (tq, LANES))

        @pl.when(jnp.logical_and(jnp.logical_not(unresolved), jnp.max(tie) > 0.0))
        def _ties():
            def body(kb, before):
                sc = sc_ref[kb]
                eq = jnp.logical_and(sc == thr, tie > 0.0)
                drop = jnp.logical_and(eq, _tie_rank(eq, before) > k_top - c_gt)
                sc_ref[kb] = jnp.where(drop, -jnp.inf, sc)
                return before + _row_count(eq)
            lax.fori_loop(0, nkb, body, jnp.zeros((tq, 1), F32))

        @pl.when(unresolved)
        def _exact():
            tk_ = _kth_key(lambda c: count_blocks(lambda sc: _sortable_key(sc) >= c), tq, k_top)
            n_ge = count_blocks(lambda sc: _sortable_key(sc) >= tk_)
            n_eq = count_blocks(lambda sc: _sortable_key(sc) == tk_)
            need = k_top - (n_ge - n_eq)

            def body(kb, before):
                sc = sc_ref[kb]
                key = _sortable_key(sc)
                eq = key == tk_
                keep_eq = jnp.logical_and(eq, jnp.logical_and(_tie_rank(eq, before) <= need, tk_ > INT_MIN))
                sc_ref[kb] = jnp.where(jnp.logical_or(key > tk_, keep_eq), sc, -jnp.inf)
                return before + _row_count(eq)
            lax.fori_loop(0, nkb, body, jnp.zeros((tq, 1), F32))
            thr_ref[...] = jnp.full((tq, LANES), F32_LOWEST, F32)
            strict_ref[...] = jnp.zeros((tq, LANES), F32)

    thr = thr_ref[...][:, 0:1]
    strict = strict_ref[...][:, 0:1]
    rows2 = A_GROUP * tq

    lane = lax.broadcasted_iota(I32, (tq, LANES), 1)
    for g0 in range(0, A_KV_HEADS, ATT_HEADS_PER_LOOP):
        heads = range(g0, g0 + ATT_HEADS_PER_LOOP)

        def att_body(kb, carry, heads=heads):
            start = pl.multiple_of(kb * tk, tk)
            bias = jnp.where(_selected(sc_ref[kb], thr, strict), 0.0, NEG_BIG)
            bias = jnp.concatenate([bias] * A_GROUP, axis=0)
            new = []
            for g, (m, l, acc) in zip(heads, carry):
                q2 = q_ref[0, A_GROUP * g:A_GROUP * (g + 1)].reshape(rows2, A_HEAD_DIM)
                s = _dot_nt(q2, k_ref[0, g, pl.ds(start, tk), :]) + bias
                m_new = jnp.maximum(m, jnp.max(s, axis=-1, keepdims=True))
                p = jnp.exp2(s - m_new)
                alpha = jnp.exp2(m - m_new)
                l = alpha * l + jnp.sum(p, axis=-1, keepdims=True)
                acc = alpha * acc + _dot(p.astype(BF16), v_ref[0, g, pl.ds(start, tk), :])
                new.append((m_new, l, acc))
            return tuple(new)
        init = tuple((jnp.full((rows2, 1), NEG_BIG, F32), jnp.zeros((rows2, 1), F32),
                      jnp.zeros((rows2, LANES), F32)) for _ in heads)
        final = lax.fori_loop(0, nkb, att_body, init)
        for g, (_, l, acc) in zip(heads, final):
            o = acc / l
            o_ref[0, :, g * LANES:(g + 1) * LANES] = jnp.where(lane < A_HEAD_DIM, o[0:tq], o[tq:2 * tq]).astype(o_ref.dtype)


def _dsa_prompt(q_hm, k_hm, v_dup, qi_hm, ki_bf, pg):
    b, _, s, _ = q_hm.shape
    tq = min(DSA_TQ, s)
    tk = min(DSA_TK, s)
    k_top = min(TOPK_MAX, s // 4)
    qb = lambda i, j: (i, 0, j, 0)
    whole = lambda i, j: (i, 0, 0, 0)
    kern = functools.partial(_dsa_prompt_kernel, tq=tq, tk=tk, k_top=k_top)
    return pl.pallas_call(
        kern,
        grid=(b, s // tq),
        in_specs=[
            pl.BlockSpec((1, A_HEADS, tq, A_HEAD_DIM), qb),
            pl.BlockSpec((1, A_KV_HEADS, s, A_HEAD_DIM), whole),
            pl.BlockSpec((1, A_KV_HEADS, s, LANES), whole),
            pl.BlockSpec((1, IDX_HEADS, tq, IDX_DIM), qb),
            pl.BlockSpec((1, s, IDX_DIM), lambda i, j: (i, 0, 0)),
            pl.BlockSpec((1, tq, PG_W), lambda i, j: (i, j, 0)),
        ],
        out_specs=pl.BlockSpec((1, tq, A_WIDTH), lambda i, j: (i, j, 0)),
        out_shape=jax.ShapeDtypeStruct((b, s, A_WIDTH), BF16),
        scratch_shapes=[pltpu.VMEM((s // tk, tq, tk), F32), pltpu.VMEM((tq, LANES), F32),
                        pltpu.VMEM((tq, LANES), F32)],
        compiler_params=_cparams(("arbitrary", "arbitrary")),
        name="dsa_prompt",
    )(q_hm, k_hm, v_dup, qi_hm, ki_bf, pg)


def _dsa_sample_kernel(pt_ref, q_ref, qi_ref, w_ref, knt_ref, vnt_ref, kint_ref, *rest, g, n_pages, t_new, k_top):
    kidx_refs, kc_refs, vc_refs = rest[0:g], rest[g:2 * g], rest[2 * g:3 * g]
    o_ref, sc_ref, kres_ref, vres_ref, bias_ref = rest[3 * g:]
    p = pl.program_id(1)
    rows = SAMPLE_ROWS
    past = n_pages * PAGE_SIZE
    w = w_ref[0][:, 2 * M_HEADS:2 * M_HEADS + IDX_HEADS] * (IDX_HEADS ** -0.5)

    def scores(ki_t):
        d = jnp.maximum(_dot(qi_ref[0], ki_t), 0.0).reshape(IDX_HEADS, rows, PAGE_SIZE)
        sc = d[0] * w[:, 0:1]
        for h in range(1, IDX_HEADS):
            sc = sc + d[h] * w[:, h:h + 1]
        return sc

    for j in range(g):
        off = pl.multiple_of((p * g + j) * PAGE_SIZE, PAGE_SIZE)
        sc_ref[:, pl.ds(off, PAGE_SIZE)] = scores(kidx_refs[j][0, 0].astype(BF16))
        for h in range(A_KV_HEADS):
            kres_ref[h, :, pl.ds(off, PAGE_SIZE)] = kc_refs[j][0, 0, h].astype(BF16)
            vres_ref[h, :, pl.ds(off, PAGE_SIZE)] = vc_refs[j][0, 0, h].astype(BF16)

    @pl.when(p == pl.num_programs(1) - 1)
    def _emit():
        t_i = lax.broadcasted_iota(I32, (rows, PAGE_SIZE), 0)
        j_i = lax.broadcasted_iota(I32, (rows, PAGE_SIZE), 1)
        vis = jnp.logical_and(j_i <= t_i, j_i < t_new)
        sc_ref[:, past:past + PAGE_SIZE] = jnp.where(vis, scores(kint_ref[0]), -jnp.inf)
        for h in range(A_KV_HEADS):
            kres_ref[h, :, past:past + PAGE_SIZE] = knt_ref[0, h]
            vres_ref[h, :, past:past + PAGE_SIZE] = vnt_ref[0, h]
        _select_dense(sc_ref[...], k_top, bias_ref)
        bias = jnp.concatenate([bias_ref[...]] * A_GROUP, axis=0)
        for h in range(A_KV_HEADS):
            s = _dot(q_ref[0, h], kres_ref[h]) + bias
            pr = jnp.exp2(s - jnp.max(s, axis=-1, keepdims=True))
            o = _dot_nt(pr.astype(BF16), vres_ref[h]) / jnp.sum(pr, axis=-1, keepdims=True)
            o_ref[0, A_GROUP * h:A_GROUP * (h + 1)] = o.reshape(A_GROUP, rows, A_HEAD_DIM).astype(o_ref.dtype)


def _dsa_sample(q_g, qi_all, w8, kn_t, vn_t, kin_t, ck_t, cv_t, cidx_t, page_table, *, layer, t_new):
    db = q_g.shape[0]
    n_pages = page_table.shape[1]
    g = PAGES_PER_STEP if n_pages % PAGES_PER_STEP == 0 else n_pages
    n_steps = n_pages // g
    n_keys = (n_pages + 1) * PAGE_SIZE
    k_top = min(TOPK_MAX, (n_pages * PAGE_SIZE + t_new) // 4)
    per_b3 = lambda b, p, pt: (b, 0, 0)
    per_b4 = lambda b, p, pt: (b, 0, 0, 0)

    def page4(j):
        return lambda b, p, pt: (layer, pt[b * n_pages + p * g + j], 0, 0)

    def page5(j):
        return lambda b, p, pt: (layer, pt[b * n_pages + p * g + j], 0, 0, 0)

    in_specs = [
        pl.BlockSpec((1, A_KV_HEADS, A_GROUP * SAMPLE_ROWS, A_HEAD_DIM), per_b4),
        pl.BlockSpec((1, IDX_HEADS * SAMPLE_ROWS, IDX_DIM), per_b3),
        pl.BlockSpec((1, SAMPLE_ROWS, PG_W), per_b3),
        pl.BlockSpec((1, A_KV_HEADS, A_HEAD_DIM, PAGE_SIZE), per_b4),
        pl.BlockSpec((1, A_KV_HEADS, A_HEAD_DIM, PAGE_SIZE), per_b4),
        pl.BlockSpec((1, IDX_DIM, PAGE_SIZE), per_b3),
    ]
    in_specs += [pl.BlockSpec((1, 1, IDX_DIM, PAGE_SIZE), page4(j)) for j in range(g)]
    in_specs += [pl.BlockSpec((1, 1, A_KV_HEADS, A_HEAD_DIM, PAGE_SIZE), page5(j)) for j in range(g)]
    in_specs += [pl.BlockSpec((1, 1, A_KV_HEADS, A_HEAD_DIM, PAGE_SIZE), page5(j)) for j in range(g)]
    grid_spec = pltpu.PrefetchScalarGridSpec(
        num_scalar_prefetch=1,
        grid=(db, n_steps),
        in_specs=in_specs,
        out_specs=pl.BlockSpec((1, A_HEADS, SAMPLE_ROWS, A_HEAD_DIM), per_b4),
        scratch_shapes=[
            pltpu.VMEM((SAMPLE_ROWS, n_keys), F32),
            pltpu.VMEM((A_KV_HEADS, A_HEAD_DIM, n_keys), BF16),
            pltpu.VMEM((A_KV_HEADS, A_HEAD_DIM, n_keys), BF16),
            pltpu.VMEM((SAMPLE_ROWS, n_keys), F32),
        ],
    )
    kern = functools.partial(_dsa_sample_kernel, g=g, n_pages=n_pages, t_new=t_new, k_top=k_top)
    return pl.pallas_call(
        kern, grid_spec=grid_spec,
        out_shape=jax.ShapeDtypeStruct((db, A_HEADS, SAMPLE_ROWS, A_HEAD_DIM), BF16),
        compiler_params=_cparams(("arbitrary", "arbitrary")),
        name="dsa_sample",
    )(page_table.reshape(-1), q_g, qi_all, w8, kn_t, vn_t, kin_t, *([cidx_t] * g), *([ck_t] * g), *([cv_t] * g))


def _sample_layouts(q_hm, qi_hm, k_o, v_o, ki_o, pg):
    db, _, t, _ = q_hm.shape
    pad_t = SAMPLE_ROWS - t
    pad_rows = lambda x: jnp.pad(x, ((0, 0), (0, 0), (0, pad_t), (0, 0)))
    q_g = pad_rows(q_hm).reshape(db, A_KV_HEADS, A_GROUP * SAMPLE_ROWS, A_HEAD_DIM)
    qi_all = pad_rows(qi_hm).reshape(db, IDX_HEADS * SAMPLE_ROWS, IDX_DIM)
    w8 = jnp.pad(pg, ((0, 0), (0, pad_t), (0, 0)))

    def heads_t(x):
        xt = x.reshape(db, t, A_KV_HEADS, A_HEAD_DIM).transpose(0, 2, 3, 1)
        return jnp.pad(xt, ((0, 0), (0, 0), (0, 0), (0, PAGE_SIZE - t))).astype(BF16)

    kin_t = jnp.pad(ki_o.transpose(0, 2, 1), ((0, 0), (0, 0), (0, PAGE_SIZE - t))).astype(BF16)
    return q_g, qi_all, w8, heads_t(k_o), heads_t(v_o), kin_t


def _finish_kernel(x_ref, hm_ref, a_ref, wo_ref, gf_ref, wgu_ref, wd_ref, y_ref):
    x1 = (x_ref[...] + _dot(hm_ref[...], wo_ref[0:M_WIDTH, :]) + _dot(a_ref[...], wo_ref[M_WIDTH:M_WIDTH + A_WIDTH, :]))
    ms = jnp.mean(x1 * x1, axis=-1, keepdims=True)
    xn = (x1 * lax.rsqrt(ms + NORM_EPS) * gf_ref[...]).astype(BF16)
    g = _dot(xn, wgu_ref[:, 0:D_FF])
    u = _dot(xn, wgu_ref[:, D_FF:2 * D_FF])
    y_ref[...] = x1 + _dot((g * jax.nn.sigmoid(g) * u).astype(BF16), wd_ref[...])


def _finish(x, hm, a, w_out, g_ffn, w_gu, w_down):
    n = x.shape[0]
    tm = min(ROW_TILE, n)
    const = lambda i: (0, 0)
    row = lambda i: (i, 0)
    return pl.pallas_call(
        _finish_kernel,
        grid=(n // tm,),
        in_specs=[
            pl.BlockSpec((tm, D_MODEL), row),
            pl.BlockSpec((tm, M_WIDTH), row),
            pl.BlockSpec((tm, A_WIDTH), row),
            pl.BlockSpec((M_WIDTH + A_WIDTH, D_MODEL), const),
            pl.BlockSpec((1, D_MODEL), const),
            pl.BlockSpec((D_MODEL, 2 * D_FF), const),
            pl.BlockSpec((D_FF, D_MODEL), const),
        ],
        out_specs=pl.BlockSpec((tm, D_MODEL), row),
        out_shape=jax.ShapeDtypeStruct((n, D_MODEL), F32),
        compiler_params=_cparams(("arbitrary",)),
        name="finish",
    )(x, hm, a, w_out, g_ffn, w_gu, w_down)


def _rope_tables(pos):
    half = A_HEAD_DIM // 2
    inv = ROPE_THETA ** (-jnp.arange(half, dtype=F32) / half)
    ang = pos.astype(F32)[:, None] * inv[None, :]
    cos, sin = jnp.cos(ang), jnp.sin(ang)
    cos_t = jnp.tile(jnp.concatenate([cos, cos], axis=-1), (1, LANES // A_HEAD_DIM))
    sin_t = jnp.tile(jnp.concatenate([-sin, sin], axis=-1), (1, LANES // A_HEAD_DIM))
    return cos_t, sin_t


def _layer_weights(l, g_mix, w_in, conv_w, b_gate, g_mout, g_q, g_k, w_out, g_ffn, w_gate_up, w_down):
    w = w_in[l]
    o_mi = PM_W
    o_aq = o_mi + 2 * M_HEADS
    o_iw = o_aq + PA_W + IDX_HEADS * IDX_DIM + IDX_DIM
    w_main = jnp.concatenate(
        [w[:, 0:PM_W], w[:, o_aq:o_iw], jnp.zeros((D_MODEL, LANES - IDX_DIM), F32)], axis=1).astype(BF16)
    w_gate = jnp.concatenate(
        [w[:, o_mi:o_aq], w[:, o_iw:o_iw + IDX_HEADS], jnp.zeros((D_MODEL, PG_W - 2 * M_HEADS - IDX_HEADS), F32)], axis=1)
    wg_hi, wg_lo = _split2(w_gate)
    bias_row = jnp.concatenate([b_gate[l], jnp.zeros((PG_W - 2 * M_HEADS,), F32)])[None, :]
    tile2 = lambda g: jnp.tile(g, LANES // A_HEAD_DIM)[None, :]
    return dict(
        g_mix=g_mix[l][None, :], w_main=w_main, wg_hi=wg_hi, wg_lo=wg_lo,
        conv_w=conv_w[l], bias_row=bias_row, g_mout=g_mout[l][None, :],
        gq=tile2(g_q[l]), gk=tile2(g_k[l]),
        w_out=w_out[l].astype(BF16), g_ffn=g_ffn[l][None, :],
        w_gu=w_gate_up[l].astype(BF16), w_down=w_down[l].astype(BF16))


def kernel(x_prompt, x_sample, cache_k, cache_v, cache_kidx, page_table, state_C, state_n, state_m, state_conv,
           g_mix, w_in, conv_w, b_gate, g_mout, g_q, g_k, w_out, g_ffn, w_gate_up, w_down):
    b, s, _ = x_prompt.shape
    db, t, _ = x_sample.shape
    depth = w_in.shape[0]
    past = page_table.shape[1] * PAGE_SIZE
    cos_p, sin_p = _rope_tables(jnp.arange(s, dtype=I32))
    cos_s, sin_s = _rope_tables(past + jnp.arange(t, dtype=I32))
    seg = (jnp.arange(LANES)[:, None] // A_HEAD_DIM == jnp.arange(LANES)[None, :] // A_HEAD_DIM).astype(BF16)
    ck_t = cache_k.transpose(0, 1, 3, 4, 2)
    cv_t = cache_v.transpose(0, 1, 3, 4, 2)
    cidx_t = cache_kidx.transpose(0, 1, 3, 2)
    xp = x_prompt.reshape(b * s, D_MODEL)
    xs = x_sample.reshape(db * t, D_MODEL)
    kp, vp, kip, cp, np_, mp, bp = [], [], [], [], [], [], []
    ks_, vs_, kis, cs, ns, ms, bs = [], [], [], [], [], [], []
    for l in range(depth):
        wl = _layer_weights(l, g_mix, w_in, conv_w, b_gate, g_mout, g_q, g_k, w_out, g_ffn, w_gate_up, w_down)
        pm, pa, pi, pg = _project(xp, wl["g_mix"], wl["w_main"], wl["wg_hi"], wl["wg_lo"])
        pm3, pg3 = pm.reshape(b, s, PM_W), pg.reshape(b, s, PG_W)
        hm, c_o, n_o, m_o, conv_o = _mlstm(pm3, pg3, wl["conv_w"], wl["bias_row"], wl["g_mout"], None)
        q_hm, k_hm, v_dup, qi_hm, ki_bf, k_o, v_o, ki_o = _aprep(
            pa.reshape(b, s, PA_W), pi.reshape(b, s, PI_W), cos_p, sin_p, wl["gq"], wl["gk"], seg)
        a = _dsa_prompt(q_hm, k_hm, v_dup, qi_hm, ki_bf, pg3)
        xp = _finish(xp, hm.reshape(b * s, M_WIDTH), a.reshape(b * s, A_WIDTH),
                     wl["w_out"], wl["g_ffn"], wl["w_gu"], wl["w_down"])
        kp.append(k_o.reshape(b, s, A_KV_HEADS, A_HEAD_DIM))
        vp.append(v_o.reshape(b, s, A_KV_HEADS, A_HEAD_DIM))
        kip.append(ki_o)
        cp.append(c_o)
        np_.append(n_o[:, :, 0, :])
        mp.append(m_o[:, :, 0, 0])
        bp.append(conv_o)

        pm, pa, pi, pg = _project(xs, wl["g_mix"], wl["w_main"], wl["wg_hi"], wl["wg_lo"])
        pg3 = pg.reshape(db, t, PG_W)
        state = (state_C[l], state_n[l][:, :, None, :],
                 jnp.broadcast_to(state_m[l][:, :, None, None], (db, M_HEADS, 1, LANES)), state_conv[l])
        hm, c_o, n_o, m_o, conv_o = _mlstm(pm.reshape(db, t, PM_W), pg3, wl["conv_w"], wl["bias_row"],
                                           wl["g_mout"], state)
        q_hm, _, _, qi_hm, _, k_o, v_o, ki_o = _aprep(
            pa.reshape(db, t, PA_W), pi.reshape(db, t, PI_W), cos_s, sin_s, wl["gq"], wl["gk"], seg)
        a = _dsa_sample(*_sample_layouts(q_hm, qi_hm, k_o, v_o, ki_o, pg3),
                        ck_t, cv_t, cidx_t, page_table, layer=l, t_new=t)
        a = a[:, :, 0:t, :].transpose(0, 2, 1, 3).reshape(db * t, A_WIDTH)
        xs = _finish(xs, hm.reshape(db * t, M_WIDTH), a, wl["w_out"], wl["g_ffn"], wl["w_gu"], wl["w_down"])
        ks_.append(k_o.reshape(db, t, A_KV_HEADS, A_HEAD_DIM))
        vs_.append(v_o.reshape(db, t, A_KV_HEADS, A_HEAD_DIM))
        kis.append(ki_o)
        cs.append(c_o)
        ns.append(n_o[:, :, 0, :])
        ms.append(m_o[:, :, 0, 0])
        bs.append(conv_o)
    return (xp.reshape(b, s, D_MODEL), xs.reshape(db, t, D_MODEL),
            jnp.stack(kp), jnp.stack(vp), jnp.stack(kip), jnp.stack(cp), jnp.stack(np_), jnp.stack(mp), jnp.stack(bp),
            jnp.stack(ks_), jnp.stack(vs_), jnp.stack(kis), jnp.stack(cs), jnp.stack(ns), jnp.stack(ms), jnp.stack(bs))
```

```python
import functools
import math

import jax
import jax.numpy as jnp
from jax import lax
from jax.experimental import pallas as pl
from jax.experimental.pallas import tpu as pltpu

F32 = jnp.float32
BF16 = jnp.bfloat16
I32 = jnp.int32

D_MODEL = 1024
M_HEADS = 4
M_HEAD_DIM = 128
M_WIDTH = M_HEADS * M_HEAD_DIM
CONV_WIDTH = 4
A_HEAD_DIM = 64
A_HEADS = 8
A_KV_HEADS = 4
A_GROUP = A_HEADS // A_KV_HEADS
A_WIDTH = A_HEADS * A_HEAD_DIM
A_KV_WIDTH = A_KV_HEADS * A_HEAD_DIM
IDX_HEADS = 8
IDX_DIM = 64
TOPK_MAX = 256
PAGE_SIZE = 128
ROPE_THETA = 10000.0
D_FF = 2816
NORM_EPS = 1e-6

LANES = 128
SUBLANES = 8
INT_MIN = -(2 ** 31)
NEG_BIG = -(2.0 ** 100)
F32_LOWEST = -3.0e38
LOG2E = math.log2(math.e)

PM_W = 4 * M_WIDTH
PA_W = A_WIDTH + 2 * A_KV_WIDTH
PI_W = IDX_HEADS * IDX_DIM + LANES
PG_W = LANES
MAIN_W = PM_W + PA_W + PI_W

ROW_TILE = 256
MLSTM_CHUNK = 256
DSA_TQ = 256
DSA_TK = 512
COUNT_ROWS = 128
ATT_HEADS_PER_LOOP = 4
SAMPLE_ROWS = SUBLANES
N_BISECT = 16
MAX_WALK = 8
VMEM_LIMIT = 56 * 1024 * 1024


def _cparams(sem):
    return pltpu.CompilerParams(dimension_semantics=sem, vmem_limit_bytes=VMEM_LIMIT)


def _split2(x):
    hi = x.astype(BF16)
    lo = (x - hi.astype(F32)).astype(BF16)
    return hi, lo


def _split3(x):
    a = x.astype(BF16)
    r = x - a.astype(F32)
    b = r.astype(BF16)
    c = (r - b.astype(F32)).astype(BF16)
    return a, b, c


def _dot(a, b):
    return jnp.dot(a, b, preferred_element_type=F32)


def _dot_nt(a, b):
    return lax.dot_general(a, b, (((1,), (1,)), ((), ())), preferred_element_type=F32)


def _proj_kernel(x_ref, g_ref, w_ref, wgh_ref, wgl_ref, pm_ref, pa_ref, pi_ref, pg_ref):
    x = x_ref[...]
    ms = jnp.mean(x * x, axis=-1, keepdims=True)
    y = x * lax.rsqrt(ms + NORM_EPS) * g_ref[...]
    yh, yl = _split2(y)
    pm_ref[...] = _dot(yh, w_ref[:, 0:PM_W])
    pa_ref[...] = _dot(yh, w_ref[:, PM_W:PM_W + PA_W])
    pi_ref[...] = _dot(yh, w_ref[:, PM_W + PA_W:MAIN_W])
    pg_ref[...] = _dot(yh, wgh_ref[...]) + _dot(yh, wgl_ref[...]) + _dot(yl, wgh_ref[...])


def _project(x, g, w_main, wg_hi, wg_lo):
    n = x.shape[0]
    tm = min(ROW_TILE, n)
    const = lambda i: (0, 0)
    return pl.pallas_call(
        _proj_kernel,
        grid=(n // tm,),
        in_specs=[
            pl.BlockSpec((tm, D_MODEL), lambda i: (i, 0)),
            pl.BlockSpec((1, D_MODEL), const),
            pl.BlockSpec((D_MODEL, MAIN_W), const),
            pl.BlockSpec((D_MODEL, PG_W), const),
            pl.BlockSpec((D_MODEL, PG_W), const),
        ],
        out_specs=[
            pl.BlockSpec((tm, PM_W), lambda i: (i, 0)),
            pl.BlockSpec((tm, PA_W), lambda i: (i, 0)),
            pl.BlockSpec((tm, PI_W), lambda i: (i, 0)),
            pl.BlockSpec((tm, PG_W), lambda i: (i, 0)),
        ],
        out_shape=[
            jax.ShapeDtypeStruct((n, PM_W), F32),
            jax.ShapeDtypeStruct((n, PA_W), F32),
            jax.ShapeDtypeStruct((n, PI_W), F32),
            jax.ShapeDtypeStruct((n, PG_W), F32),
        ],
        compiler_params=_cparams(("arbitrary",)),
        name="proj",
    )(x, g, w_main, wg_hi, wg_lo)


def _log_sigmoid(x):
    return jnp.minimum(x, 0.0) - jnp.log1p(jnp.exp(-jnp.abs(x)))


def _mlstm_kernel(*refs, lc, tb, tv, nc, has_state):
    if has_state:
        (pm_ref, pg_ref, cw_ref, bias_ref, gm_ref, c0_ref, n0_ref, m0_ref, conv0_ref,
         h_ref, c_out_ref, n_out_ref, m_out_ref, conv_out_ref,
         c_scr, n_scr, m_scr, ext_scr, u_scr) = refs
    else:
        (pm_ref, pg_ref, cw_ref, bias_ref, gm_ref,
         h_ref, c_out_ref, n_out_ref, m_out_ref, conv_out_ref,
         c_scr, n_scr, m_scr, ext_scr, u_scr) = refs
    c = pl.program_id(1)
    qk_w = 2 * M_WIDTH

    @pl.when(c == 0)
    def _init():
        ext_scr[0:SUBLANES, :] = jnp.zeros((SUBLANES, qk_w), F32)
        if has_state:
            c_scr[...] = c0_ref[0]
            n_scr[...] = n0_ref[0]
            m_scr[...] = m0_ref[0]
            ext_scr[SUBLANES - (CONV_WIDTH - 1):SUBLANES, :] = conv0_ref[0]
        else:
            c_scr[...] = jnp.zeros(c_scr.shape, F32)
            n_scr[...] = jnp.zeros(n_scr.shape, F32)
            m_scr[...] = jnp.zeros(m_scr.shape, F32)

    if tb < lc:
        u_scr[...] = jnp.zeros(u_scr.shape, F32)
        u_scr[0:tb, 0:PM_W] = pm_ref[0]
        u_scr[0:tb, PM_W:PM_W + PG_W] = pg_ref[0]
        pm = u_scr[:, 0:PM_W]
        gates = u_scr[:, PM_W:PM_W + PG_W]
    else:
        pm = pm_ref[0]
        gates = pg_ref[0]

    ext_scr[SUBLANES:SUBLANES + lc, :] = pm[:, 0:qk_w]
    cw = cw_ref[...]
    qk = ext_scr[pl.ds(SUBLANES - 3, lc), :] * cw[0:1, :]
    for j in range(1, CONV_WIDTH):
        qk = qk + ext_scr[pl.ds(SUBLANES - 3 + j, lc), :] * cw[j:j + 1, :]
    new_tail = ext_scr[pl.ds(tv, SUBLANES), :]
    qk = qk * jax.nn.sigmoid(qk)

    a = gates + bias_ref[...]
    ig = a
    lf = _log_sigmoid(a)
    if tv < lc:
        valid = lax.broadcasted_iota(I32, (lc, LANES), 0) < tv
        ig = jnp.where(valid, ig, NEG_BIG)
        lf = jnp.where(valid, lf, 0.0)
    ig_t = ig.T[0:SUBLANES, :]
    lf_t = lf.T[0:SUBLANES, :]

    r_i = lax.broadcasted_iota(I32, (lc, lc), 0)
    c_i = lax.broadcasted_iota(I32, (lc, lc), 1)
    causal = c_i <= r_i
    tril = jnp.where(causal, 1.0, 0.0).astype(BF16)
    triu = jnp.where(r_i <= c_i, 1.0, 0.0).astype(BF16)
    b_cols = sum(_dot(tril, p) for p in _split3(lf))
    b_rows = sum(_dot(p, triu) for p in _split3(lf_t))

    outs = []
    for h in range(M_HEADS):
        lo, hi = h * M_HEAD_DIM, (h + 1) * M_HEAD_DIM
        q = qk[:, lo:hi]
        k = qk[:, M_WIDTH + lo:M_WIDTH + hi] * (M_HEAD_DIM ** -0.5)
        v = pm[:, 2 * M_WIDTH + lo:2 * M_WIDTH + hi]
        og = pm[:, 3 * M_WIDTH + lo:3 * M_WIDTH + hi]
        b_col = b_cols[:, M_HEADS + h:M_HEADS + h + 1]
        i_col = ig[:, h:h + 1]
        b_row = b_rows[M_HEADS + h:M_HEADS + h + 1, :]
        i_row = ig_t[h:h + 1, :]
        m_prev = m_scr[h][:, 0:1]
        c_h = c_scr[h]
        n_h = n_scr[h]

        log_d = jnp.where(causal, b_col - b_row + i_row, NEG_BIG)
        inter = b_col + m_prev
        m_t = jnp.maximum(jnp.max(log_d, axis=-1, keepdims=True), inter)
        w_inter = jnp.exp(inter - m_t)
        d_mat = jnp.exp(log_d - m_t)
        qb = q.astype(BF16)
        s = _dot_nt(qb, k.astype(BF16)) * d_mat
        num = w_inter * _dot_nt(qb, c_h.astype(BF16)) + _dot(s.astype(BF16), v.astype(BF16))
        nq = w_inter * jnp.sum(q * n_h, axis=-1, keepdims=True) + jnp.sum(s, axis=-1, keepdims=True)
        hh = num / jnp.maximum(jnp.abs(nq), jnp.exp(-m_t))
        ms = jnp.mean(hh * hh, axis=-1, keepdims=True)
        hn = hh * lax.rsqrt(ms + NORM_EPS) * gm_ref[:, lo:hi]
        outs.append(jax.nn.sigmoid(og) * hn)

        b_last = b_cols[lc - 1:lc, M_HEADS + h:M_HEADS + h + 1]
        log_w = b_last - b_col + i_col
        m_new = jnp.maximum(b_last + m_prev, jnp.max(log_w, axis=0, keepdims=True))
        decay = jnp.exp(b_last + m_prev - m_new)
        kw = k * jnp.exp(log_w - m_new)
        c_scr[h] = decay * c_h + _dot(v.T.astype(BF16), kw.astype(BF16))
        n_scr[h] = decay * n_h + jnp.sum(kw, axis=0, keepdims=True)
        m_scr[h] = jnp.broadcast_to(m_new, (1, LANES))

    out = jnp.concatenate(outs, axis=-1)
    h_ref[0] = out[0:tb].astype(h_ref.dtype)
    ext_scr[0:SUBLANES, :] = new_tail

    @pl.when(c == nc - 1)
    def _fin():
        c_out_ref[0] = c_scr[...]
        n_out_ref[0] = n_scr[...]
        m_out_ref[0] = m_scr[...]
        conv_out_ref[0] = new_tail[SUBLANES - (CONV_WIDTH - 1):SUBLANES, :]


def _mlstm(pm, pg, conv_w, bias_row, g_mout, state):
    b, t, _ = pm.shape
    if t >= MLSTM_CHUNK:
        lc, tb, tv = MLSTM_CHUNK, MLSTM_CHUNK, MLSTM_CHUNK
    else:
        lc, tb, tv = LANES, t, t
    nc = max(t // lc, 1)
    has_state = state is not None
    qk_w = 2 * M_WIDTH
    const2 = lambda i, j: (0, 0)
    per_b4 = lambda i, j: (i, 0, 0, 0)
    in_specs = [
        pl.BlockSpec((1, tb, PM_W), lambda i, j: (i, j, 0)),
        pl.BlockSpec((1, tb, PG_W), lambda i, j: (i, j, 0)),
        pl.BlockSpec((CONV_WIDTH, qk_w), const2),
        pl.BlockSpec((1, PG_W), const2),
        pl.BlockSpec((1, M_WIDTH), const2),
    ]
    args = [pm, pg, conv_w, bias_row, g_mout]
    if has_state:
        in_specs += [
            pl.BlockSpec((1, M_HEADS, M_HEAD_DIM, M_HEAD_DIM), per_b4),
            pl.BlockSpec((1, M_HEADS, 1, M_HEAD_DIM), per_b4),
            pl.BlockSpec((1, M_HEADS, 1, LANES), per_b4),
            pl.BlockSpec((1, CONV_WIDTH - 1, qk_w), lambda i, j: (i, 0, 0)),
        ]
        args += list(state)
    out_specs = [
        pl.BlockSpec((1, tb, M_WIDTH), lambda i, j: (i, j, 0)),
        pl.BlockSpec((1, M_HEADS, M_HEAD_DIM, M_HEAD_DIM), per_b4),
        pl.BlockSpec((1, M_HEADS, 1, M_HEAD_DIM), per_b4),
        pl.BlockSpec((1, M_HEADS, 1, LANES), per_b4),
        pl.BlockSpec((1, CONV_WIDTH - 1, qk_w), lambda i, j: (i, 0, 0)),
    ]
    out_shape = [
        jax.ShapeDtypeStruct((b, t, M_WIDTH), BF16),
        jax.ShapeDtypeStruct((b, M_HEADS, M_HEAD_DIM, M_HEAD_DIM), F32),
        jax.ShapeDtypeStruct((b, M_HEADS, 1, M_HEAD_DIM), F32),
        jax.ShapeDtypeStruct((b, M_HEADS, 1, LANES), F32),
        jax.ShapeDtypeStruct((b, CONV_WIDTH - 1, qk_w), F32),
    ]
    scratch = [
        pltpu.VMEM((M_HEADS, M_HEAD_DIM, M_HEAD_DIM), F32),
        pltpu.VMEM((M_HEADS, 1, M_HEAD_DIM), F32),
        pltpu.VMEM((M_HEADS, 1, LANES), F32),
        pltpu.VMEM((lc + 2 * SUBLANES, qk_w), F32),
        pltpu.VMEM((lc, PM_W + PG_W), F32),
    ]
    kern = functools.partial(_mlstm_kernel, lc=lc, tb=tb, tv=tv, nc=nc, has_state=has_state)
    return pl.pallas_call(
        kern, grid=(b, nc), in_specs=in_specs, out_specs=out_specs, out_shape=out_shape,
        scratch_shapes=scratch, compiler_params=_cparams(("arbitrary", "arbitrary")), name="mlstm",
    )(*args)


def _swap_halves(x):
    lane = lax.broadcasted_iota(I32, x.shape, 1)
    first = (lane % A_HEAD_DIM) < (A_HEAD_DIM // 2)
    return jnp.where(first, pltpu.roll(x, LANES - A_HEAD_DIM // 2, 1), pltpu.roll(x, A_HEAD_DIM // 2, 1))


def _rope(x, cos, sin):
    return x * cos + _swap_halves(x) * sin


def _head_rms(x, seg, g):
    hi, lo = _split2(x * x)
    ss = _dot(hi, seg) + _dot(lo, seg)
    return x * lax.rsqrt(ss * (1.0 / A_HEAD_DIM) + NORM_EPS) * g


def _aprep_kernel(pa_ref, pi_ref, cos_ref, sin_ref, gq_ref, gk_ref, seg_ref,
                  q_ref, kh_ref, vd_ref, qi_ref, kib_ref, k_out_ref, v_out_ref, ki_out_ref):
    cos = cos_ref[...]
    sin = sin_ref[...]
    seg = seg_ref[...]
    lane = lax.broadcasted_iota(I32, cos.shape, 1)
    low = lane < A_HEAD_DIM
    half = A_HEAD_DIM

    def split_heads(blk):
        return blk[:, 0:half], pltpu.roll(blk, half, 1)[:, 0:half]

    for j in range(A_WIDTH // LANES):
        blk = _rope(_head_rms(pa_ref[0, :, j * LANES:(j + 1) * LANES], seg, gq_ref[...]), cos, sin)
        blk = blk * (A_HEAD_DIM ** -0.5 * LOG2E)
        h0, h1 = split_heads(blk)
        q_ref[0, 2 * j] = h0.astype(BF16)
        q_ref[0, 2 * j + 1] = h1.astype(BF16)
    for j in range(A_KV_WIDTH // LANES):
        blk = _rope(_head_rms(pa_ref[0, :, A_WIDTH + j * LANES:A_WIDTH + (j + 1) * LANES], seg, gk_ref[...]), cos, sin)
        k_out_ref[0, :, j * LANES:(j + 1) * LANES] = blk
        h0, h1 = split_heads(blk)
        kh_ref[0, 2 * j] = h0.astype(BF16)
        kh_ref[0, 2 * j + 1] = h1.astype(BF16)
        vb = pa_ref[0, :, A_WIDTH + A_KV_WIDTH + j * LANES:A_WIDTH + A_KV_WIDTH + (j + 1) * LANES]
        v_out_ref[0, :, j * LANES:(j + 1) * LANES] = vb
        vr = pltpu.roll(vb, half, 1)
        ones_col = jnp.where(lane == half, 1.0, 0.0)
        vd_ref[0, 2 * j] = jnp.where(low, vb, ones_col).astype(BF16)
        vd_ref[0, 2 * j + 1] = jnp.where(low, vr, ones_col).astype(BF16)
    for j in range(IDX_HEADS * IDX_DIM // LANES):
        blk = _rope(pi_ref[0, :, j * LANES:(j + 1) * LANES], cos, sin) * (IDX_DIM ** -0.5)
        h0, h1 = split_heads(blk)
        qi_ref[0, 2 * j] = h0.astype(BF16)
        qi_ref[0, 2 * j + 1] = h1.astype(BF16)
    kblk = _rope(pi_ref[0, :, IDX_HEADS * IDX_DIM:IDX_HEADS * IDX_DIM + LANES], cos, sin)[:, 0:half]
    ki_out_ref[0] = kblk
    kib_ref[0] = kblk.astype(BF16)


def _aprep(pa, pi, cos, sin, gq, gk, seg):
    b, t, _ = pa.shape
    tm = min(ROW_TILE, t)
    c2 = lambda i, j: (0, 0)
    row3 = lambda i, j: (i, j, 0)
    hm = lambda i, j: (i, 0, j, 0)
    return pl.pallas_call(
        _aprep_kernel,
        grid=(b, t // tm),
        in_specs=[
            pl.BlockSpec((1, tm, PA_W), row3),
            pl.BlockSpec((1, tm, PI_W), row3),
            pl.BlockSpec((tm, LANES), lambda i, j: (j, 0)),
            pl.BlockSpec((tm, LANES), lambda i, j: (j, 0)),
            pl.BlockSpec((1, LANES), c2),
            pl.BlockSpec((1, LANES), c2),
            pl.BlockSpec((LANES, LANES), c2),
        ],
        out_specs=[
            pl.BlockSpec((1, A_HEADS, tm, A_HEAD_DIM), hm),
            pl.BlockSpec((1, A_KV_HEADS, tm, A_HEAD_DIM), hm),
            pl.BlockSpec((1, A_KV_HEADS, tm, LANES), hm),
            pl.BlockSpec((1, IDX_HEADS, tm, IDX_DIM), hm),
            pl.BlockSpec((1, tm, IDX_DIM), row3),
            pl.BlockSpec((1, tm, A_KV_WIDTH), row3),
            pl.BlockSpec((1, tm, A_KV_WIDTH), row3),
            pl.BlockSpec((1, tm, IDX_DIM), row3),
        ],
        out_shape=[
            jax.ShapeDtypeStruct((b, A_HEADS, t, A_HEAD_DIM), BF16),
            jax.ShapeDtypeStruct((b, A_KV_HEADS, t, A_HEAD_DIM), BF16),
            jax.ShapeDtypeStruct((b, A_KV_HEADS, t, LANES), BF16),
            jax.ShapeDtypeStruct((b, IDX_HEADS, t, IDX_DIM), BF16),
            jax.ShapeDtypeStruct((b, t, IDX_DIM), BF16),
            jax.ShapeDtypeStruct((b, t, A_KV_WIDTH), F32),
            jax.ShapeDtypeStruct((b, t, A_KV_WIDTH), F32),
            jax.ShapeDtypeStruct((b, t, IDX_DIM), F32),
        ],
        compiler_params=_cparams(("arbitrary", "arbitrary")),
        name="aprep",
    )(pa, pi, cos, sin, gq, gk, seg)


def _sortable_key(score):
    bits = lax.bitcast_convert_type(score, I32)
    key = bits ^ ((bits >> 31) & 0x7FFFFFFF)
    key = jnp.where(key == -1, 0, key)
    return jnp.where(score == -jnp.inf, INT_MIN, key)


def _fold_lanes(x):
    acc = x[:, 0:LANES]
    for j in range(1, x.shape[1] // LANES):
        acc = acc + x[:, j * LANES:(j + 1) * LANES]
    return acc


def _fold_lanes_min(x):
    acc = x[:, 0:LANES]
    for j in range(1, x.shape[1] // LANES):
        acc = jnp.minimum(acc, x[:, j * LANES:(j + 1) * LANES])
    return acc


def _fold_lanes_max(x):
    acc = x[:, 0:LANES]
    for j in range(1, x.shape[1] // LANES):
        acc = jnp.maximum(acc, x[:, j * LANES:(j + 1) * LANES])
    return acc


def _row_count(mask):
    return jnp.sum(_fold_lanes(jnp.where(mask, 1.0, 0.0)), axis=-1, keepdims=True)


def _bisect(count_ge, lo, hi, k_top):
    def body(_, c):
        lo, hi = c
        mid = lo + (hi - lo) * 0.5
        ge = count_ge(mid) >= k_top
        return jnp.where(ge, mid, lo), jnp.where(ge, hi, mid)
    return lax.fori_loop(0, N_BISECT, body, (lo, hi))[0]


def _selected(sc, thr, strict):
    at_least = jnp.where(strict > 0.0, F32_LOWEST, thr)
    above = jnp.where(strict > 0.0, thr, F32_LOWEST)
    return jnp.logical_and(sc >= at_least, sc > above)


def _walk_up(lo, c_lo, k_top, min_selected, count_gt):
    def cond(c):
        return jnp.logical_and(c[5], c[6] < MAX_WALK)

    def body(c):
        thr, strict, c_sel, c_gt, tie, _, it = c
        active = jnp.logical_and(c_sel > k_top, tie == 0.0)
        v = min_selected(thr, strict)
        c_above = count_gt(v)
        adv = jnp.logical_and(active, c_above >= k_top)
        stop = jnp.logical_and(active, c_above < k_top)
        again = jnp.max(jnp.where(jnp.logical_and(adv, c_above > k_top), 1.0, 0.0)) > 0.0
        return (jnp.where(active, v, thr), jnp.where(adv, 1.0, jnp.where(stop, 0.0, strict)),
                jnp.where(adv, c_above, c_sel), jnp.where(stop, c_above, c_gt), jnp.where(stop, 1.0, tie),
                again, it + 1)

    zero = jnp.zeros_like(lo)
    thr, strict, _, c_gt, tie, pending, _ = lax.while_loop(
        cond, body, (lo, zero, c_lo, zero, zero, jnp.max(c_lo) > k_top, jnp.int32(0)))
    return thr, strict, c_gt, tie, pending


def _kth_key(count_ge_key, rows, k_top):
    def body(it, thr):
        cand = thr ^ lax.shift_left(jnp.int32(1), 31 - it)
        return jnp.where(count_ge_key(cand) >= k_top, cand, thr)
    return lax.fori_loop(0, 32, body, jnp.full((rows, 1), INT_MIN, I32))


def _tie_rank(eq, before):
    n = eq.shape[1]
    r_i = lax.broadcasted_iota(I32, (n, n), 0)
    c_i = lax.broadcasted_iota(I32, (n, n), 1)
    triu = jnp.where(r_i <= c_i, 1.0, 0.0).astype(BF16)
    return before + _dot(jnp.where(eq, 1.0, 0.0).astype(BF16), triu)


def _dense_tie_ranks(eq):
    before = jnp.zeros((eq.shape[0], 1), F32)
    ranks = []
    for j in range(eq.shape[1] // LANES):
        eq_j = eq[:, j * LANES:(j + 1) * LANES]
        ranks.append(_tie_rank(eq_j, before))
        before = before + _row_count(eq_j)
    return jnp.concatenate(ranks, axis=-1)


def _select_dense(sc, k_top, bias_ref):
    rows, n = sc.shape
    lo0 = jnp.min(jnp.where(sc == -jnp.inf, jnp.inf, sc), axis=-1, keepdims=True)
    hi0 = jnp.max(sc, axis=-1, keepdims=True)
    count_ge = lambda thr: _row_count(sc >= thr)
    lo = _bisect(count_ge, lo0, hi0, k_top)
    c_lo = count_ge(lo)
    bias_ref[...] = jnp.where(sc >= lo, 0.0, NEG_BIG)

    @pl.when(jnp.max(c_lo) > k_top)
    def _refine():
        min_selected = lambda thr, strict: jnp.min(
            _fold_lanes_min(jnp.where(_selected(sc, thr, strict), sc, jnp.inf)), axis=-1, keepdims=True)
        thr, strict, c_gt, tie, unresolved = _walk_up(lo, c_lo, k_top, min_selected, lambda v: _row_count(sc > v))

        @pl.when(jnp.logical_not(unresolved))
        def _ties():
            eq = jnp.logical_and(sc == thr, tie > 0.0)
            drop = jnp.logical_and(eq, _dense_tie_ranks(eq) > k_top - c_gt)
            keep = jnp.logical_and(_selected(sc, thr, strict), jnp.logical_not(drop))
            bias_ref[...] = jnp.where(keep, 0.0, NEG_BIG)

        @pl.when(unresolved)
        def _exact():
            key = _sortable_key(sc)
            tk_ = _kth_key(lambda c: _row_count(key >= c), rows, k_top)
            eq = key == tk_
            need = k_top - (_row_count(key >= tk_) - _row_count(eq))
            keep_eq = jnp.logical_and(eq, jnp.logical_and(_dense_tie_ranks(eq) <= need, tk_ > INT_MIN))
            bias_ref[...] = jnp.where(jnp.logical_or(key > tk_, keep_eq), 0.0, NEG_BIG)


def _dsa_prompt_kernel(q_ref, k_ref, v_ref, qi_ref, ki_ref, pg_ref, o_ref, sc_ref, thr_ref, strict_ref,
                       *, tq, tk, k_top):
    i = pl.program_id(1)
    r0 = i * tq
    nkb = (r0 + tq + tk - 1) // tk
    row = r0 + lax.broadcasted_iota(I32, (tq, tk), 0)
    col0 = lax.broadcasted_iota(I32, (tq, tk), 1)

    qi = qi_ref[0].reshape(IDX_HEADS * tq, IDX_DIM)
    w = pg_ref[0][:, 2 * M_HEADS:2 * M_HEADS + IDX_HEADS] * (IDX_HEADS ** -0.5)

    def score_body(kb, carry):
        lo, hi = carry
        start = pl.multiple_of(kb * tk, tk)
        d = _dot_nt(qi, ki_ref[0, pl.ds(start, tk), :])
        d = jnp.maximum(d, 0.0).reshape(IDX_HEADS, tq, tk)
        sc = d[0] * w[:, 0:1]
        for h in range(1, IDX_HEADS):
            sc = sc + d[h] * w[:, h:h + 1]
        vis = col0 + kb * tk <= row
        sc_vis = jnp.where(vis, sc, -jnp.inf)
        sc_ref[kb] = sc_vis
        lo = jnp.minimum(lo, _fold_lanes_min(jnp.where(vis, sc, jnp.inf)))
        hi = jnp.maximum(hi, _fold_lanes_max(sc_vis))
        return lo, hi
    lo0, hi0 = lax.fori_loop(0, nkb, score_body,
                             (jnp.full((tq, LANES), jnp.inf, F32), jnp.full((tq, LANES), -jnp.inf, F32)))
    lo0 = jnp.min(lo0, axis=-1, keepdims=True)
    hi0 = jnp.max(hi0, axis=-1, keepdims=True)

    def count_blocks(pred):
        def body(kb, acc):
            return acc + _fold_lanes(jnp.where(pred(sc_ref[kb]), 1.0, 0.0))
        acc = lax.fori_loop(0, nkb, body, jnp.zeros((tq, LANES), F32))
        return jnp.sum(acc, axis=-1, keepdims=True)

    def min_selected(thr, strict):
        def body(kb, acc):
            sc = sc_ref[kb]
            return jnp.minimum(acc, _fold_lanes_min(jnp.where(_selected(sc, thr, strict), sc, jnp.inf)))
        acc = lax.fori_loop(0, nkb, body, jnp.full((tq, LANES), jnp.inf, F32))
        return jnp.min(acc, axis=-1, keepdims=True)

    ones_sq = jnp.ones((LANES, LANES), BF16)

    def count_ge_rep(thr_rep):
        parts = []
        for r in range(0, tq, COUNT_ROWS):
            t = thr_rep[r:r + COUNT_ROWS]

            def body(kb, acc, r=r, t=t):
                for j in range(tk // LANES):
                    tile = sc_ref[kb, r:r + COUNT_ROWS, j * LANES:(j + 1) * LANES]
                    acc = acc + jnp.where(tile >= t, 1.0, 0.0)
                return acc
            parts.append(lax.fori_loop(0, nkb, body, jnp.zeros((COUNT_ROWS, LANES), F32)))
        return _dot(jnp.concatenate(parts, axis=0).astype(BF16), ones_sq)

    count_ge = lambda thr: count_blocks(lambda sc: sc >= thr)
    lo = _bisect(count_ge_rep, jnp.broadcast_to(lo0, (tq, LANES)), jnp.broadcast_to(hi0, (tq, LANES)), k_top)[:, 0:1]
    c_lo = count_ge(lo)
    thr_ref[...] = jnp.broadcast_to(lo, (tq, LANES))
    strict_ref[...] = jnp.zeros((tq, LANES), F32)

    @pl.when(jnp.max(c_lo) > k_top)
    def _refine():
        thr, strict, c_gt, tie, unresolved = _walk_up(
            lo, c_lo, k_top, min_selected, lambda v: count_blocks(lambda sc: sc > v))
        thr_ref[...] = jnp.broadcast_to(thr, (tq, LANES))
        strict_ref[...] = jnp.broadcast_to(strict, (tq, LANES))

        @pl.when(jnp.logical_and(jnp.logical_not(unresolved), jnp.max(tie) > 0.0))
        def _ties():
            def body(kb, before):
                sc = sc_ref[kb]
                eq = jnp.logical_and(sc == thr, tie > 0.0)
                drop = jnp.logical_and(eq, _tie_rank(eq, before) > k_top - c_gt)
                sc_ref[kb] = jnp.where(drop, -jnp.inf, sc)
                return before + _row_count(eq)
            lax.fori_loop(0, nkb, body, jnp.zeros((tq, 1), F32))

        @pl.when(unresolved)
        def _exact():
            tk_ = _kth_key(lambda c: count_blocks(lambda sc: _sortable_key(sc) >= c), tq, k_top)
            n_ge = count_blocks(lambda sc: _sortable_key(sc) >= tk_)
            n_eq = count_blocks(lambda sc: _sortable_key(sc) == tk_)
            need = k_top - (n_ge - n_eq)

            def body(kb, before):
                sc = sc_ref[kb]
                key = _sortable_key(sc)
                eq = key == tk_
                keep_eq = jnp.logical_and(eq, jnp.logical_and(_tie_rank(eq, before) <= need, tk_ > INT_MIN))
                sc_ref[kb] = jnp.where(jnp.logical_or(key > tk_, keep_eq), sc, -jnp.inf)
                return before + _row_count(eq)
            lax.fori_loop(0, nkb, body, jnp.zeros((tq, 1), F32))
            thr_ref[...] = jnp.full((tq, LANES), F32_LOWEST, F32)
            strict_ref[...] = jnp.zeros((tq, LANES), F32)

    thr = thr_ref[...][:, 0:1]
    strict = strict_ref[...][:, 0:1]
    rows2 = A_GROUP * tq

    lane = lax.broadcasted_iota(I32, (tq, LANES), 1)
    for g0 in range(0, A_KV_HEADS, ATT_HEADS_PER_LOOP):
        heads = range(g0, g0 + ATT_HEADS_PER_LOOP)

        def att_body(kb, carry, heads=heads):
            start = pl.multiple_of(kb * tk, tk)
            bias = jnp.where(_selected(sc_ref[kb], thr, strict), 0.0, NEG_BIG).astype(BF16)
            bias = jnp.concatenate([bias] * A_GROUP, axis=0)
            new = []
            for g, (m, acc) in zip(heads, carry):
                q2 = q_ref[0, A_GROUP * g:A_GROUP * (g + 1)].reshape(rows2, A_HEAD_DIM)
                s = _dot_nt(q2, k_ref[0, g, pl.ds(start, tk), :]).astype(BF16) + bias
                m_new = jnp.maximum(m, jnp.max(s, axis=-1, keepdims=True).astype(F32))
                p = jnp.exp2(s - m_new.astype(BF16))
                acc = jnp.exp2(m - m_new) * acc + _dot(p, v_ref[0, g, pl.ds(start, tk), :])
                new.append((m_new, acc))
            return tuple(new)
        init = tuple((jnp.full((rows2, 1), NEG_BIG, F32), jnp.zeros((rows2, LANES), F32)) for _ in heads)
        final = lax.fori_loop(0, nkb, att_body, init)
        for g, (_, acc) in zip(heads, final):
            o = acc / acc[:, A_HEAD_DIM:A_HEAD_DIM + 1]
            slab = jnp.where(lane < A_HEAD_DIM, o[0:tq], pltpu.roll(o[tq:2 * tq], A_HEAD_DIM, 1))
            o_ref[0, :, g * LANES:(g + 1) * LANES] = slab.astype(o_ref.dtype)


def _dsa_prompt(q_hm, k_hm, v_dup, qi_hm, ki_bf, pg):
    b, _, s, _ = q_hm.shape
    tq = min(DSA_TQ, s)
    tk = min(DSA_TK, s)
    k_top = min(TOPK_MAX, s // 4)
    qb = lambda i, j: (i, 0, j, 0)
    whole = lambda i, j: (i, 0, 0, 0)
    kern = functools.partial(_dsa_prompt_kernel, tq=tq, tk=tk, k_top=k_top)
    return pl.pallas_call(
        kern,
        grid=(b, s // tq),
        in_specs=[
            pl.BlockSpec((1, A_HEADS, tq, A_HEAD_DIM), qb),
            pl.BlockSpec((1, A_KV_HEADS, s, A_HEAD_DIM), whole),
            pl.BlockSpec((1, A_KV_HEADS, s, LANES), whole),
            pl.BlockSpec((1, IDX_HEADS, tq, IDX_DIM), qb),
            pl.BlockSpec((1, s, IDX_DIM), lambda i, j: (i, 0, 0)),
            pl.BlockSpec((1, tq, PG_W), lambda i, j: (i, j, 0)),
        ],
        out_specs=pl.BlockSpec((1, tq, A_WIDTH), lambda i, j: (i, j, 0)),
        out_shape=jax.ShapeDtypeStruct((b, s, A_WIDTH), BF16),
        scratch_shapes=[pltpu.VMEM((s // tk, tq, tk), F32), pltpu.VMEM((tq, LANES), F32),
                        pltpu.VMEM((tq, LANES), F32)],
        compiler_params=_cparams(("arbitrary", "arbitrary")),
        name="dsa_prompt",
    )(q_hm, k_hm, v_dup, qi_hm, ki_bf, pg)


def _page_copies(pt_ref, cidx_hbm, ck_hbm, cv_hbm, kidx_buf, k_buf, v_buf, sems, seq, slot, *, layer, n_pages):
    copies = []
    for p in range(n_pages):
        page = pt_ref[seq * n_pages + p]
        cols = pl.ds(p * PAGE_SIZE, PAGE_SIZE)
        copies.append(pltpu.make_async_copy(cidx_hbm.at[layer, page], kidx_buf.at[slot, :, cols], sems.at[slot, 0]))
        copies.append(pltpu.make_async_copy(ck_hbm.at[layer, page], k_buf.at[slot, :, :, cols], sems.at[slot, 1]))
        copies.append(pltpu.make_async_copy(cv_hbm.at[layer, page], v_buf.at[slot, :, :, cols], sems.at[slot, 2]))
    return copies


def _dsa_sample_kernel(pt_ref, q_ref, qi_ref, w_ref, knt_ref, vnt_ref, kint_ref, cidx_hbm, ck_hbm, cv_hbm,
                       o_ref, kidx_buf, k_buf, v_buf, bias_ref, sems, *, layer, n_pages, t_new, k_top):
    b = pl.program_id(0)
    slot = b % 2
    rows = SAMPLE_ROWS
    copies = functools.partial(_page_copies, pt_ref, cidx_hbm, ck_hbm, cv_hbm, kidx_buf, k_buf, v_buf, sems,
                               layer=layer, n_pages=n_pages)

    @pl.when(b == 0)
    def _first():
        for c in copies(0, 0):
            c.start()

    @pl.when(b + 1 < pl.num_programs(0))
    def _next():
        for c in copies(b + 1, 1 - slot):
            c.start()

    for c in copies(b, slot):
        c.wait()

    w = w_ref[0][:, 2 * M_HEADS:2 * M_HEADS + IDX_HEADS] * (IDX_HEADS ** -0.5)

    def scores(ki_t):
        d = jnp.maximum(_dot(qi_ref[0], ki_t), 0.0).reshape(IDX_HEADS, rows, ki_t.shape[1])
        sc = d[0] * w[:, 0:1]
        for h in range(1, IDX_HEADS):
            sc = sc + d[h] * w[:, h:h + 1]
        return sc

    t_i = lax.broadcasted_iota(I32, (rows, PAGE_SIZE), 0)
    j_i = lax.broadcasted_iota(I32, (rows, PAGE_SIZE), 1)
    vis = jnp.logical_and(j_i <= t_i, j_i < t_new)
    sc = jnp.concatenate([scores(kidx_buf[slot].astype(BF16)),
                          jnp.where(vis, scores(kint_ref[0]), -jnp.inf)], axis=-1)
    _select_dense(sc, k_top, bias_ref)
    bias = jnp.concatenate([bias_ref[...]] * A_GROUP, axis=0)
    for h in range(A_KV_HEADS):
        qh = q_ref[0, h]
        s = jnp.concatenate([_dot(qh, k_buf[slot, h].astype(BF16)), _dot(qh, knt_ref[0, h])], axis=-1) + bias
        pr = jnp.exp2(s - jnp.max(s, axis=-1, keepdims=True))
        prb = pr.astype(BF16)
        past = n_pages * PAGE_SIZE
        o = _dot_nt(prb[:, 0:past], v_buf[slot, h].astype(BF16)) + _dot_nt(prb[:, past:], vnt_ref[0, h])
        o = o / jnp.sum(pr, axis=-1, keepdims=True)
        o_ref[0, A_GROUP * h:A_GROUP * (h + 1)] = o.reshape(A_GROUP, rows, A_HEAD_DIM).astype(o_ref.dtype)


def _dsa_sample(q_g, qi_all, w8, kn_t, vn_t, kin_t, ck_t, cv_t, cidx_t, page_table, *, layer, t_new):
    db = q_g.shape[0]
    n_pages = page_table.shape[1]
    past = n_pages * PAGE_SIZE
    k_top = min(TOPK_MAX, (past + t_new) // 4)
    per_b3 = lambda b, pt: (b, 0, 0)
    per_b4 = lambda b, pt: (b, 0, 0, 0)
    hbm = pl.BlockSpec(memory_space=pl.ANY)
    grid_spec = pltpu.PrefetchScalarGridSpec(
        num_scalar_prefetch=1,
        grid=(db,),
        in_specs=[
            pl.BlockSpec((1, A_KV_HEADS, A_GROUP * SAMPLE_ROWS, A_HEAD_DIM), per_b4),
            pl.BlockSpec((1, IDX_HEADS * SAMPLE_ROWS, IDX_DIM), per_b3),
            pl.BlockSpec((1, SAMPLE_ROWS, PG_W), per_b3),
            pl.BlockSpec((1, A_KV_HEADS, A_HEAD_DIM, PAGE_SIZE), per_b4),
            pl.BlockSpec((1, A_KV_HEADS, A_HEAD_DIM, PAGE_SIZE), per_b4),
            pl.BlockSpec((1, IDX_DIM, PAGE_SIZE), per_b3),
            hbm, hbm, hbm,
        ],
        out_specs=pl.BlockSpec((1, A_HEADS, SAMPLE_ROWS, A_HEAD_DIM), per_b4),
        scratch_shapes=[
            pltpu.VMEM((2, IDX_DIM, past), F32),
            pltpu.VMEM((2, A_KV_HEADS, A_HEAD_DIM, past), F32),
            pltpu.VMEM((2, A_KV_HEADS, A_HEAD_DIM, past), F32),
            pltpu.VMEM((SAMPLE_ROWS, past + PAGE_SIZE), F32),
            pltpu.SemaphoreType.DMA((2, 3)),
        ],
    )
    kern = functools.partial(_dsa_sample_kernel, layer=layer, n_pages=n_pages, t_new=t_new, k_top=k_top)
    return pl.pallas_call(
        kern, grid_spec=grid_spec,
        out_shape=jax.ShapeDtypeStruct((db, A_HEADS, SAMPLE_ROWS, A_HEAD_DIM), BF16),
        compiler_params=_cparams(("arbitrary",)),
        name="dsa_sample",
    )(page_table.reshape(-1), q_g, qi_all, w8, kn_t, vn_t, kin_t, cidx_t, ck_t, cv_t)


def _sample_layouts(q_hm, qi_hm, k_o, v_o, ki_o, pg):
    db, _, t, _ = q_hm.shape
    pad_t = SAMPLE_ROWS - t
    pad_rows = lambda x: jnp.pad(x, ((0, 0), (0, 0), (0, pad_t), (0, 0)))
    q_g = pad_rows(q_hm).reshape(db, A_KV_HEADS, A_GROUP * SAMPLE_ROWS, A_HEAD_DIM)
    qi_all = pad_rows(qi_hm).reshape(db, IDX_HEADS * SAMPLE_ROWS, IDX_DIM)
    w8 = jnp.pad(pg, ((0, 0), (0, pad_t), (0, 0)))

    def heads_t(x):
        xt = x.reshape(db, t, A_KV_HEADS, A_HEAD_DIM).transpose(0, 2, 3, 1)
        return jnp.pad(xt, ((0, 0), (0, 0), (0, 0), (0, PAGE_SIZE - t))).astype(BF16)

    kin_t = jnp.pad(ki_o.transpose(0, 2, 1), ((0, 0), (0, 0), (0, PAGE_SIZE - t))).astype(BF16)
    return q_g, qi_all, w8, heads_t(k_o), heads_t(v_o), kin_t


def _finish_kernel(x_ref, hm_ref, a_ref, wo_ref, gf_ref, wgu_ref, wd_ref, y_ref):
    x1 = (x_ref[...] + _dot(hm_ref[...], wo_ref[0:M_WIDTH, :]) + _dot(a_ref[...], wo_ref[M_WIDTH:M_WIDTH + A_WIDTH, :]))
    ms = jnp.mean(x1 * x1, axis=-1, keepdims=True)
    xn = (x1 * lax.rsqrt(ms + NORM_EPS) * gf_ref[...]).astype(BF16)
    g = _dot(xn, wgu_ref[:, 0:D_FF])
    u = _dot(xn, wgu_ref[:, D_FF:2 * D_FF])
    y_ref[...] = x1 + _dot((g * jax.nn.sigmoid(g) * u).astype(BF16), wd_ref[...])


def _finish(x, hm, a, w_out, g_ffn, w_gu, w_down):
    n = x.shape[0]
    tm = min(ROW_TILE, n)
    const = lambda i: (0, 0)
    row = lambda i: (i, 0)
    return pl.pallas_call(
        _finish_kernel,
        grid=(n // tm,),
        in_specs=[
            pl.BlockSpec((tm, D_MODEL), row),
            pl.BlockSpec((tm, M_WIDTH), row),
            pl.BlockSpec((tm, A_WIDTH), row),
            pl.BlockSpec((M_WIDTH + A_WIDTH, D_MODEL), const),
            pl.BlockSpec((1, D_MODEL), const),
            pl.BlockSpec((D_MODEL, 2 * D_FF), const),
            pl.BlockSpec((D_FF, D_MODEL), const),
        ],
        out_specs=pl.BlockSpec((tm, D_MODEL), row),
        out_shape=jax.ShapeDtypeStruct((n, D_MODEL), F32),
        compiler_params=_cparams(("arbitrary",)),
        name="finish",
    )(x, hm, a, w_out, g_ffn, w_gu, w_down)


def _rope_tables(pos):
    half = A_HEAD_DIM // 2
    inv = ROPE_THETA ** (-jnp.arange(half, dtype=F32) / half)
    ang = pos.astype(F32)[:, None] * inv[None, :]
    cos, sin = jnp.cos(ang), jnp.sin(ang)
    cos_t = jnp.tile(jnp.concatenate([cos, cos], axis=-1), (1, LANES // A_HEAD_DIM))
    sin_t = jnp.tile(jnp.concatenate([-sin, sin], axis=-1), (1, LANES // A_HEAD_DIM))
    return cos_t, sin_t


def _layer_weights(l, g_mix, w_in, conv_w, b_gate, g_mout, g_q, g_k, w_out, g_ffn, w_gate_up, w_down):
    w = w_in[l]
    o_mi = PM_W
    o_aq = o_mi + 2 * M_HEADS
    o_iw = o_aq + PA_W + IDX_HEADS * IDX_DIM + IDX_DIM
    w_main = jnp.concatenate(
        [w[:, 0:PM_W], w[:, o_aq:o_iw], jnp.zeros((D_MODEL, LANES - IDX_DIM), F32)], axis=1).astype(BF16)
    w_gate = jnp.concatenate(
        [w[:, o_mi:o_aq], w[:, o_iw:o_iw + IDX_HEADS], jnp.zeros((D_MODEL, PG_W - 2 * M_HEADS - IDX_HEADS), F32)], axis=1)
    wg_hi, wg_lo = _split2(w_gate)
    bias_row = jnp.concatenate([b_gate[l], jnp.zeros((PG_W - 2 * M_HEADS,), F32)])[None, :]
    tile2 = lambda g: jnp.tile(g, LANES // A_HEAD_DIM)[None, :]
    return dict(
        g_mix=g_mix[l][None, :], w_main=w_main, wg_hi=wg_hi, wg_lo=wg_lo,
        conv_w=conv_w[l], bias_row=bias_row, g_mout=g_mout[l][None, :],
        gq=tile2(g_q[l]), gk=tile2(g_k[l]),
        w_out=w_out[l].astype(BF16), g_ffn=g_ffn[l][None, :],
        w_gu=w_gate_up[l].astype(BF16), w_down=w_down[l].astype(BF16))


def kernel(x_prompt, x_sample, cache_k, cache_v, cache_kidx, page_table, state_C, state_n, state_m, state_conv,
           g_mix, w_in, conv_w, b_gate, g_mout, g_q, g_k, w_out, g_ffn, w_gate_up, w_down):
    b, s, _ = x_prompt.shape
    db, t, _ = x_sample.shape
    depth = w_in.shape[0]
    past = page_table.shape[1] * PAGE_SIZE
    cos_p, sin_p = _rope_tables(jnp.arange(s, dtype=I32))
    cos_s, sin_s = _rope_tables(past + jnp.arange(t, dtype=I32))
    seg = (jnp.arange(LANES)[:, None] // A_HEAD_DIM == jnp.arange(LANES)[None, :] // A_HEAD_DIM).astype(BF16)
    ck_t = cache_k.transpose(0, 1, 3, 4, 2)
    cv_t = cache_v.transpose(0, 1, 3, 4, 2)
    cidx_t = cache_kidx.transpose(0, 1, 3, 2)
    xp = x_prompt.reshape(b * s, D_MODEL)
    xs = x_sample.reshape(db * t, D_MODEL)
    kp, vp, kip, cp, np_, mp, bp = [], [], [], [], [], [], []
    ks_, vs_, kis, cs, ns, ms, bs = [], [], [], [], [], [], []
    for l in range(depth):
        wl = _layer_weights(l, g_mix, w_in, conv_w, b_gate, g_mout, g_q, g_k, w_out, g_ffn, w_gate_up, w_down)
        pm, pa, pi, pg = _project(xp, wl["g_mix"], wl["w_main"], wl["wg_hi"], wl["wg_lo"])
        pm3, pg3 = pm.reshape(b, s, PM_W), pg.reshape(b, s, PG_W)
        hm, c_o, n_o, m_o, conv_o = _mlstm(pm3, pg3, wl["conv_w"], wl["bias_row"], wl["g_mout"], None)
        q_hm, k_hm, v_dup, qi_hm, ki_bf, k_o, v_o, ki_o = _aprep(
            pa.reshape(b, s, PA_W), pi.reshape(b, s, PI_W), cos_p, sin_p, wl["gq"], wl["gk"], seg)
        a = _dsa_prompt(q_hm, k_hm, v_dup, qi_hm, ki_bf, pg3)
        xp = _finish(xp, hm.reshape(b * s, M_WIDTH), a.reshape(b * s, A_WIDTH),
                     wl["w_out"], wl["g_ffn"], wl["w_gu"], wl["w_down"])
        kp.append(k_o.reshape(b, s, A_KV_HEADS, A_HEAD_DIM))
        vp.append(v_o.reshape(b, s, A_KV_HEADS, A_HEAD_DIM))
        kip.append(ki_o)
        cp.append(c_o)
        np_.append(n_o[:, :, 0, :])
        mp.append(m_o[:, :, 0, 0])
        bp.append(conv_o)

        pm, pa, pi, pg = _project(xs, wl["g_mix"], wl["w_main"], wl["wg_hi"], wl["wg_lo"])
        pg3 = pg.reshape(db, t, PG_W)
        state = (state_C[l], state_n[l][:, :, None, :],
                 jnp.broadcast_to(state_m[l][:, :, None, None], (db, M_HEADS, 1, LANES)), state_conv[l])
        hm, c_o, n_o, m_o, conv_o = _mlstm(pm.reshape(db, t, PM_W), pg3, wl["conv_w"], wl["bias_row"],
                                           wl["g_mout"], state)
        q_hm, _, _, qi_hm, _, k_o, v_o, ki_o = _aprep(
            pa.reshape(db, t, PA_W), pi.reshape(db, t, PI_W), cos_s, sin_s, wl["gq"], wl["gk"], seg)
        a = _dsa_sample(*_sample_layouts(q_hm, qi_hm, k_o, v_o, ki_o, pg3),
                        ck_t, cv_t, cidx_t, page_table, layer=l, t_new=t)
        a = a[:, :, 0:t, :].transpose(0, 2, 1, 3).reshape(db * t, A_WIDTH)
        xs = _finish(xs, hm.reshape(db * t, M_WIDTH), a, wl["w_out"], wl["g_ffn"], wl["w_gu"], wl["w_down"])
        ks_.append(k_o.reshape(db, t, A_KV_HEADS, A_HEAD_DIM))
        vs_.append(v_o.reshape(db, t, A_KV_HEADS, A_HEAD_DIM))
        kis.append(ki_o)
        cs.append(c_o)
        ns.append(n_o[:, :, 0, :])
        ms.append(m_o[:, :, 0, 0])
        bs.append(conv_o)
    return (xp.reshape(b, s, D_MODEL), xs.reshape(db, t, D_MODEL),
            jnp.stack(kp), jnp.stack(vp), jnp.stack(kip), jnp.stack(cp), jnp.stack(np_), jnp.stack(mp), jnp.stack(bp),
            jnp.stack(ks_), jnp.stack(vs_), jnp.stack(kis), jnp.stack(cs), jnp.stack(ns), jnp.stack(ms), jnp.stack(bs))
```

```python
import functools
import math

import jax
import jax.numpy as jnp
from jax import lax
from jax.experimental import pallas as pl
from jax.experimental.pallas import tpu as pltpu

F32 = jnp.float32
BF16 = jnp.bfloat16
I32 = jnp.int32

D_MODEL = 1024
M_HEADS = 4
M_HEAD_DIM = 128
M_WIDTH = M_HEADS * M_HEAD_DIM
CONV_WIDTH = 4
A_HEAD_DIM = 64
A_HEADS = 8
A_KV_HEADS = 4
A_GROUP = A_HEADS // A_KV_HEADS
A_WIDTH = A_HEADS * A_HEAD_DIM
A_KV_WIDTH = A_KV_HEADS * A_HEAD_DIM
IDX_HEADS = 8
IDX_DIM = 64
TOPK_MAX = 256
PAGE_SIZE = 128
ROPE_THETA = 10000.0
D_FF = 2816
NORM_EPS = 1e-6

LANES = 128
SUBLANES = 8
INT_MIN = -(2 ** 31)
NEG_BIG = -(2.0 ** 100)
F32_LOWEST = -3.0e38
LOG2E = math.log2(math.e)

PM_W = 4 * M_WIDTH
PA_W = A_WIDTH + 2 * A_KV_WIDTH
PI_W = IDX_HEADS * IDX_DIM + LANES
PG_W = LANES
MAIN_W = PM_W + PA_W + PI_W

ROW_TILE = 256
MLSTM_CHUNK = 256
DSA_TQ = 256
DSA_TK = 512
COUNT_ROWS = 128
ATT_HEADS_PER_LOOP = 4
SAMPLE_ROWS = SUBLANES
N_BISECT = 16
MAX_WALK = 8
VMEM_LIMIT = 56 * 1024 * 1024


def _cparams(sem):
    return pltpu.CompilerParams(dimension_semantics=sem, vmem_limit_bytes=VMEM_LIMIT)


def _split2(x):
    hi = x.astype(BF16)
    lo = (x - hi.astype(F32)).astype(BF16)
    return hi, lo


def _split3(x):
    a = x.astype(BF16)
    r = x - a.astype(F32)
    b = r.astype(BF16)
    c = (r - b.astype(F32)).astype(BF16)
    return a, b, c


def _dot(a, b):
    return jnp.dot(a, b, preferred_element_type=F32)


def _dot_nt(a, b):
    return lax.dot_general(a, b, (((1,), (1,)), ((), ())), preferred_element_type=F32)


def _proj_kernel(x_ref, g_ref, w_ref, wgh_ref, wgl_ref, pm_ref, pa_ref, pi_ref, pg_ref):
    x = x_ref[...]
    ms = jnp.mean(x * x, axis=-1, keepdims=True)
    y = x * lax.rsqrt(ms + NORM_EPS) * g_ref[...]
    yh, yl = _split2(y)
    pm_ref[...] = _dot(yh, w_ref[:, 0:PM_W])
    pa_ref[...] = _dot(yh, w_ref[:, PM_W:PM_W + PA_W])
    pi_ref[...] = _dot(yh, w_ref[:, PM_W + PA_W:MAIN_W])
    pg_ref[...] = _dot(yh, wgh_ref[...]) + _dot(yh, wgl_ref[...]) + _dot(yl, wgh_ref[...])


def _project(x, g, w_main, wg_hi, wg_lo):
    n = x.shape[0]
    tm = min(ROW_TILE, n)
    const = lambda i: (0, 0)
    return pl.pallas_call(
        _proj_kernel,
        grid=(n // tm,),
        in_specs=[
            pl.BlockSpec((tm, D_MODEL), lambda i: (i, 0)),
            pl.BlockSpec((1, D_MODEL), const),
            pl.BlockSpec((D_MODEL, MAIN_W), const),
            pl.BlockSpec((D_MODEL, PG_W), const),
            pl.BlockSpec((D_MODEL, PG_W), const),
        ],
        out_specs=[
            pl.BlockSpec((tm, PM_W), lambda i: (i, 0)),
            pl.BlockSpec((tm, PA_W), lambda i: (i, 0)),
            pl.BlockSpec((tm, PI_W), lambda i: (i, 0)),
            pl.BlockSpec((tm, PG_W), lambda i: (i, 0)),
        ],
        out_shape=[
            jax.ShapeDtypeStruct((n, PM_W), F32),
            jax.ShapeDtypeStruct((n, PA_W), F32),
            jax.ShapeDtypeStruct((n, PI_W), F32),
            jax.ShapeDtypeStruct((n, PG_W), F32),
        ],
        compiler_params=_cparams(("arbitrary",)),
        name="proj",
    )(x, g, w_main, wg_hi, wg_lo)


def _log_sigmoid(x):
    return jnp.minimum(x, 0.0) - jnp.log1p(jnp.exp(-jnp.abs(x)))


def _mlstm_kernel(*refs, lc, tb, tv, nc, has_state):
    if has_state:
        (pm_ref, pg_ref, cw_ref, bias_ref, gm_ref, c0_ref, n0_ref, m0_ref, conv0_ref,
         h_ref, c_out_ref, n_out_ref, m_out_ref, conv_out_ref,
         c_scr, n_scr, m_scr, ext_scr, u_scr) = refs
    else:
        (pm_ref, pg_ref, cw_ref, bias_ref, gm_ref,
         h_ref, c_out_ref, n_out_ref, m_out_ref, conv_out_ref,
         c_scr, n_scr, m_scr, ext_scr, u_scr) = refs
    c = pl.program_id(1)
    qk_w = 2 * M_WIDTH

    @pl.when(c == 0)
    def _init():
        ext_scr[0:SUBLANES, :] = jnp.zeros((SUBLANES, qk_w), F32)
        if has_state:
            c_scr[...] = c0_ref[0]
            n_scr[...] = n0_ref[0]
            m_scr[...] = m0_ref[0]
            ext_scr[SUBLANES - (CONV_WIDTH - 1):SUBLANES, :] = conv0_ref[0]
        else:
            c_scr[...] = jnp.zeros(c_scr.shape, F32)
            n_scr[...] = jnp.zeros(n_scr.shape, F32)
            m_scr[...] = jnp.zeros(m_scr.shape, F32)

    if tb < lc:
        u_scr[...] = jnp.zeros(u_scr.shape, F32)
        u_scr[0:tb, 0:PM_W] = pm_ref[0]
        u_scr[0:tb, PM_W:PM_W + PG_W] = pg_ref[0]
        pm = u_scr[:, 0:PM_W]
        gates = u_scr[:, PM_W:PM_W + PG_W]
    else:
        pm = pm_ref[0]
        gates = pg_ref[0]

    ext_scr[SUBLANES:SUBLANES + lc, :] = pm[:, 0:qk_w]
    cw = cw_ref[...]
    qk = ext_scr[pl.ds(SUBLANES - 3, lc), :] * cw[0:1, :]
    for j in range(1, CONV_WIDTH):
        qk = qk + ext_scr[pl.ds(SUBLANES - 3 + j, lc), :] * cw[j:j + 1, :]
    new_tail = ext_scr[pl.ds(tv, SUBLANES), :]
    qk = qk * jax.nn.sigmoid(qk)

    a = gates + bias_ref[...]
    ig = a
    lf = _log_sigmoid(a)
    if tv < lc:
        valid = lax.broadcasted_iota(I32, (lc, LANES), 0) < tv
        ig = jnp.where(valid, ig, NEG_BIG)
        lf = jnp.where(valid, lf, 0.0)
    ig_t = ig.T[0:SUBLANES, :]
    lf_t = lf.T[0:SUBLANES, :]

    r_i = lax.broadcasted_iota(I32, (lc, lc), 0)
    c_i = lax.broadcasted_iota(I32, (lc, lc), 1)
    causal = c_i <= r_i
    tril = jnp.where(causal, 1.0, 0.0).astype(BF16)
    triu = jnp.where(r_i <= c_i, 1.0, 0.0).astype(BF16)
    b_cols = sum(_dot(tril, p) for p in _split3(lf))
    b_rows = sum(_dot(p, triu) for p in _split3(lf_t))

    outs = []
    for h in range(M_HEADS):
        lo, hi = h * M_HEAD_DIM, (h + 1) * M_HEAD_DIM
        q = qk[:, lo:hi]
        k = qk[:, M_WIDTH + lo:M_WIDTH + hi] * (M_HEAD_DIM ** -0.5)
        v = pm[:, 2 * M_WIDTH + lo:2 * M_WIDTH + hi]
        og = pm[:, 3 * M_WIDTH + lo:3 * M_WIDTH + hi]
        b_col = b_cols[:, M_HEADS + h:M_HEADS + h + 1]
        i_col = ig[:, h:h + 1]
        b_row = b_rows[M_HEADS + h:M_HEADS + h + 1, :]
        i_row = ig_t[h:h + 1, :]
        m_prev = m_scr[h][:, 0:1]
        c_h = c_scr[h]
        n_h = n_scr[h]

        log_d = jnp.where(causal, b_col - b_row + i_row, NEG_BIG)
        inter = b_col + m_prev
        m_t = jnp.maximum(jnp.max(log_d, axis=-1, keepdims=True), inter)
        w_inter = jnp.exp(inter - m_t)
        d_mat = jnp.exp(log_d - m_t)
        qb = q.astype(BF16)
        s = _dot_nt(qb, k.astype(BF16)) * d_mat
        num = w_inter * _dot_nt(qb, c_h.astype(BF16)) + _dot(s.astype(BF16), v.astype(BF16))
        nq = w_inter * jnp.sum(q * n_h, axis=-1, keepdims=True) + jnp.sum(s, axis=-1, keepdims=True)
        hh = num / jnp.maximum(jnp.abs(nq), jnp.exp(-m_t))
        ms = jnp.mean(hh * hh, axis=-1, keepdims=True)
        hn = hh * lax.rsqrt(ms + NORM_EPS) * gm_ref[:, lo:hi]
        outs.append(jax.nn.sigmoid(og) * hn)

        b_last = b_cols[lc - 1:lc, M_HEADS + h:M_HEADS + h + 1]
        log_w = b_last - b_col + i_col
        m_new = jnp.maximum(b_last + m_prev, jnp.max(log_w, axis=0, keepdims=True))
        decay = jnp.exp(b_last + m_prev - m_new)
        kw = k * jnp.exp(log_w - m_new)
        c_scr[h] = decay * c_h + _dot(v.T.astype(BF16), kw.astype(BF16))
        n_scr[h] = decay * n_h + jnp.sum(kw, axis=0, keepdims=True)
        m_scr[h] = jnp.broadcast_to(m_new, (1, LANES))

    out = jnp.concatenate(outs, axis=-1)
    h_ref[0] = out[0:tb].astype(h_ref.dtype)
    ext_scr[0:SUBLANES, :] = new_tail

    @pl.when(c == nc - 1)
    def _fin():
        c_out_ref[0] = c_scr[...]
        n_out_ref[0] = n_scr[...]
        m_out_ref[0] = m_scr[...]
        conv_out_ref[0] = new_tail[SUBLANES - (CONV_WIDTH - 1):SUBLANES, :]


def _mlstm(pm, pg, conv_w, bias_row, g_mout, state):
    b, t, _ = pm.shape
    if t >= MLSTM_CHUNK:
        lc, tb, tv = MLSTM_CHUNK, MLSTM_CHUNK, MLSTM_CHUNK
    else:
        lc, tb, tv = LANES, t, t
    nc = max(t // lc, 1)
    has_state = state is not None
    qk_w = 2 * M_WIDTH
    const2 = lambda i, j: (0, 0)
    per_b4 = lambda i, j: (i, 0, 0, 0)
    in_specs = [
        pl.BlockSpec((1, tb, PM_W), lambda i, j: (i, j, 0)),
        pl.BlockSpec((1, tb, PG_W), lambda i, j: (i, j, 0)),
        pl.BlockSpec((CONV_WIDTH, qk_w), const2),
        pl.BlockSpec((1, PG_W), const2),
        pl.BlockSpec((1, M_WIDTH), const2),
    ]
    args = [pm, pg, conv_w, bias_row, g_mout]
    if has_state:
        in_specs += [
            pl.BlockSpec((1, M_HEADS, M_HEAD_DIM, M_HEAD_DIM), per_b4),
            pl.BlockSpec((1, M_HEADS, 1, M_HEAD_DIM), per_b4),
            pl.BlockSpec((1, M_HEADS, 1, LANES), per_b4),
            pl.BlockSpec((1, CONV_WIDTH - 1, qk_w), lambda i, j: (i, 0, 0)),
        ]
        args += list(state)
    out_specs = [
        pl.BlockSpec((1, tb, M_WIDTH), lambda i, j: (i, j, 0)),
        pl.BlockSpec((1, M_HEADS, M_HEAD_DIM, M_HEAD_DIM), per_b4),
        pl.BlockSpec((1, M_HEADS, 1, M_HEAD_DIM), per_b4),
        pl.BlockSpec((1, M_HEADS, 1, LANES), per_b4),
        pl.BlockSpec((1, CONV_WIDTH - 1, qk_w), lambda i, j: (i, 0, 0)),
    ]
    out_shape = [
        jax.ShapeDtypeStruct((b, t, M_WIDTH), BF16),
        jax.ShapeDtypeStruct((b, M_HEADS, M_HEAD_DIM, M_HEAD_DIM), F32),
        jax.ShapeDtypeStruct((b, M_HEADS, 1, M_HEAD_DIM), F32),
        jax.ShapeDtypeStruct((b, M_HEADS, 1, LANES), F32),
        jax.ShapeDtypeStruct((b, CONV_WIDTH - 1, qk_w), F32),
    ]
    scratch = [
        pltpu.VMEM((M_HEADS, M_HEAD_DIM, M_HEAD_DIM), F32),
        pltpu.VMEM((M_HEADS, 1, M_HEAD_DIM), F32),
        pltpu.VMEM((M_HEADS, 1, LANES), F32),
        pltpu.VMEM((lc + 2 * SUBLANES, qk_w), F32),
        pltpu.VMEM((lc, PM_W + PG_W), F32),
    ]
    kern = functools.partial(_mlstm_kernel, lc=lc, tb=tb, tv=tv, nc=nc, has_state=has_state)
    return pl.pallas_call(
        kern, grid=(b, nc), in_specs=in_specs, out_specs=out_specs, out_shape=out_shape,
        scratch_shapes=scratch, compiler_params=_cparams(("arbitrary", "arbitrary")), name="mlstm",
    )(*args)


def _swap_halves(x):
    lane = lax.broadcasted_iota(I32, x.shape, 1)
    first = (lane % A_HEAD_DIM) < (A_HEAD_DIM // 2)
    return jnp.where(first, pltpu.roll(x, LANES - A_HEAD_DIM // 2, 1), pltpu.roll(x, A_HEAD_DIM // 2, 1))


def _rope(x, cos, sin):
    return x * cos + _swap_halves(x) * sin


def _head_rms(x, seg, g):
    hi, lo = _split2(x * x)
    ss = _dot(hi, seg) + _dot(lo, seg)
    return x * lax.rsqrt(ss * (1.0 / A_HEAD_DIM) + NORM_EPS) * g


def _aprep_kernel(pa_ref, pi_ref, cos_ref, sin_ref, gq_ref, gk_ref, seg_ref,
                  q_ref, kh_ref, vd_ref, qi_ref, kib_ref, k_out_ref, v_out_ref, ki_out_ref):
    cos = cos_ref[...]
    sin = sin_ref[...]
    seg = seg_ref[...]
    lane = lax.broadcasted_iota(I32, cos.shape, 1)
    low = lane < A_HEAD_DIM
    half = A_HEAD_DIM

    def split_heads(blk):
        return blk[:, 0:half], pltpu.roll(blk, half, 1)[:, 0:half]

    for j in range(A_WIDTH // LANES):
        blk = _rope(_head_rms(pa_ref[0, :, j * LANES:(j + 1) * LANES], seg, gq_ref[...]), cos, sin)
        blk = blk * (A_HEAD_DIM ** -0.5 * LOG2E)
        h0, h1 = split_heads(blk)
        q_ref[0, 2 * j] = h0.astype(BF16)
        q_ref[0, 2 * j + 1] = h1.astype(BF16)
    for j in range(A_KV_WIDTH // LANES):
        blk = _rope(_head_rms(pa_ref[0, :, A_WIDTH + j * LANES:A_WIDTH + (j + 1) * LANES], seg, gk_ref[...]), cos, sin)
        k_out_ref[0, :, j * LANES:(j + 1) * LANES] = blk
        h0, h1 = split_heads(blk)
        kh_ref[0, 2 * j] = h0.astype(BF16)
        kh_ref[0, 2 * j + 1] = h1.astype(BF16)
        vb = pa_ref[0, :, A_WIDTH + A_KV_WIDTH + j * LANES:A_WIDTH + A_KV_WIDTH + (j + 1) * LANES]
        v_out_ref[0, :, j * LANES:(j + 1) * LANES] = vb
        vr = pltpu.roll(vb, half, 1)
        ones_col = jnp.where(lane == half, 1.0, 0.0)
        vd_ref[0, 2 * j] = jnp.where(low, vb, ones_col).astype(BF16)
        vd_ref[0, 2 * j + 1] = jnp.where(low, vr, ones_col).astype(BF16)
    for j in range(IDX_HEADS * IDX_DIM // LANES):
        blk = _rope(pi_ref[0, :, j * LANES:(j + 1) * LANES], cos, sin) * (IDX_DIM ** -0.5)
        h0, h1 = split_heads(blk)
        qi_ref[0, 2 * j] = h0.astype(BF16)
        qi_ref[0, 2 * j + 1] = h1.astype(BF16)
    kblk = _rope(pi_ref[0, :, IDX_HEADS * IDX_DIM:IDX_HEADS * IDX_DIM + LANES], cos, sin)[:, 0:half]
    ki_out_ref[0] = kblk
    kib_ref[0] = kblk.astype(BF16)


def _aprep(pa, pi, cos, sin, gq, gk, seg):
    b, t, _ = pa.shape
    tm = min(ROW_TILE, t)
    c2 = lambda i, j: (0, 0)
    row3 = lambda i, j: (i, j, 0)
    hm = lambda i, j: (i, 0, j, 0)
    return pl.pallas_call(
        _aprep_kernel,
        grid=(b, t // tm),
        in_specs=[
            pl.BlockSpec((1, tm, PA_W), row3),
            pl.BlockSpec((1, tm, PI_W), row3),
            pl.BlockSpec((tm, LANES), lambda i, j: (j, 0)),
            pl.BlockSpec((tm, LANES), lambda i, j: (j, 0)),
            pl.BlockSpec((1, LANES), c2),
            pl.BlockSpec((1, LANES), c2),
            pl.BlockSpec((LANES, LANES), c2),
        ],
        out_specs=[
            pl.BlockSpec((1, A_HEADS, tm, A_HEAD_DIM), hm),
            pl.BlockSpec((1, A_KV_HEADS, tm, A_HEAD_DIM), hm),
            pl.BlockSpec((1, A_KV_HEADS, tm, LANES), hm),
            pl.BlockSpec((1, IDX_HEADS, tm, IDX_DIM), hm),
            pl.BlockSpec((1, tm, IDX_DIM), row3),
            pl.BlockSpec((1, tm, A_KV_WIDTH), row3),
            pl.BlockSpec((1, tm, A_KV_WIDTH), row3),
            pl.BlockSpec((1, tm, IDX_DIM), row3),
        ],
        out_shape=[
            jax.ShapeDtypeStruct((b, A_HEADS, t, A_HEAD_DIM), BF16),
            jax.ShapeDtypeStruct((b, A_KV_HEADS, t, A_HEAD_DIM), BF16),
            jax.ShapeDtypeStruct((b, A_KV_HEADS, t, LANES), BF16),
            jax.ShapeDtypeStruct((b, IDX_HEADS, t, IDX_DIM), BF16),
            jax.ShapeDtypeStruct((b, t, IDX_DIM), BF16),
            jax.ShapeDtypeStruct((b, t, A_KV_WIDTH), F32),
            jax.ShapeDtypeStruct((b, t, A_KV_WIDTH), F32),
            jax.ShapeDtypeStruct((b, t, IDX_DIM), F32),
        ],
        compiler_params=_cparams(("arbitrary", "arbitrary")),
        name="aprep",
    )(pa, pi, cos, sin, gq, gk, seg)


def _sortable_key(score):
    bits = lax.bitcast_convert_type(score, I32)
    key = bits ^ ((bits >> 31) & 0x7FFFFFFF)
    key = jnp.where(key == -1, 0, key)
    return jnp.where(score == -jnp.inf, INT_MIN, key)


def _fold_lanes(x):
    acc = x[:, 0:LANES]
    for j in range(1, x.shape[1] // LANES):
        acc = acc + x[:, j * LANES:(j + 1) * LANES]
    return acc


def _fold_lanes_min(x):
    acc = x[:, 0:LANES]
    for j in range(1, x.shape[1] // LANES):
        acc = jnp.minimum(acc, x[:, j * LANES:(j + 1) * LANES])
    return acc


def _fold_lanes_max(x):
    acc = x[:, 0:LANES]
    for j in range(1, x.shape[1] // LANES):
        acc = jnp.maximum(acc, x[:, j * LANES:(j + 1) * LANES])
    return acc


def _row_count(mask):
    return jnp.sum(_fold_lanes(jnp.where(mask, 1.0, 0.0)), axis=-1, keepdims=True)


def _bisect(count_ge, lo, hi, k_top):
    def body(_, c):
        lo, hi = c
        mid = lo + (hi - lo) * 0.5
        ge = count_ge(mid) >= k_top
        return jnp.where(ge, mid, lo), jnp.where(ge, hi, mid)
    return lax.fori_loop(0, N_BISECT, body, (lo, hi))[0]


def _selected(sc, thr, strict):
    at_least = jnp.where(strict > 0.0, F32_LOWEST, thr)
    above = jnp.where(strict > 0.0, thr, F32_LOWEST)
    return jnp.logical_and(sc >= at_least, sc > above)


def _walk_up(lo, c_lo, k_top, min_selected, count_gt):
    def cond(c):
        return jnp.logical_and(c[5], c[6] < MAX_WALK)

    def body(c):
        thr, strict, c_sel, c_gt, tie, _, it = c
        active = jnp.logical_and(c_sel > k_top, tie == 0.0)
        v = min_selected(thr, strict)
        c_above = count_gt(v)
        adv = jnp.logical_and(active, c_above >= k_top)
        stop = jnp.logical_and(active, c_above < k_top)
        again = jnp.max(jnp.where(jnp.logical_and(adv, c_above > k_top), 1.0, 0.0)) > 0.0
        return (jnp.where(active, v, thr), jnp.where(adv, 1.0, jnp.where(stop, 0.0, strict)),
                jnp.where(adv, c_above, c_sel), jnp.where(stop, c_above, c_gt), jnp.where(stop, 1.0, tie),
                again, it + 1)

    zero = jnp.zeros_like(lo)
    thr, strict, _, c_gt, tie, pending, _ = lax.while_loop(
        cond, body, (lo, zero, c_lo, zero, zero, jnp.max(c_lo) > k_top, jnp.int32(0)))
    return thr, strict, c_gt, tie, pending


def _kth_key(count_ge_key, rows, k_top):
    def body(it, thr):
        cand = thr ^ lax.shift_left(jnp.int32(1), 31 - it)
        return jnp.where(count_ge_key(cand) >= k_top, cand, thr)
    return lax.fori_loop(0, 32, body, jnp.full((rows, 1), INT_MIN, I32))


def _tie_rank(eq, before):
    n = eq.shape[1]
    r_i = lax.broadcasted_iota(I32, (n, n), 0)
    c_i = lax.broadcasted_iota(I32, (n, n), 1)
    triu = jnp.where(r_i <= c_i, 1.0, 0.0).astype(BF16)
    return before + _dot(jnp.where(eq, 1.0, 0.0).astype(BF16), triu)


def _dense_tie_ranks(eq):
    before = jnp.zeros((eq.shape[0], 1), F32)
    ranks = []
    for j in range(eq.shape[1] // LANES):
        eq_j = eq[:, j * LANES:(j + 1) * LANES]
        ranks.append(_tie_rank(eq_j, before))
        before = before + _row_count(eq_j)
    return jnp.concatenate(ranks, axis=-1)


def _select_dense(sc, k_top, bias_ref):
    rows, n = sc.shape
    lo0 = jnp.min(jnp.where(sc == -jnp.inf, jnp.inf, sc), axis=-1, keepdims=True)
    hi0 = jnp.max(sc, axis=-1, keepdims=True)
    count_ge = lambda thr: _row_count(sc >= thr)
    lo = _bisect(count_ge, lo0, hi0, k_top)
    c_lo = count_ge(lo)
    bias_ref[...] = jnp.where(sc >= lo, 0.0, NEG_BIG)

    @pl.when(jnp.max(c_lo) > k_top)
    def _refine():
        min_selected = lambda thr, strict: jnp.min(
            _fold_lanes_min(jnp.where(_selected(sc, thr, strict), sc, jnp.inf)), axis=-1, keepdims=True)
        thr, strict, c_gt, tie, unresolved = _walk_up(lo, c_lo, k_top, min_selected, lambda v: _row_count(sc > v))

        @pl.when(jnp.logical_not(unresolved))
        def _ties():
            eq = jnp.logical_and(sc == thr, tie > 0.0)
            drop = jnp.logical_and(eq, _dense_tie_ranks(eq) > k_top - c_gt)
            keep = jnp.logical_and(_selected(sc, thr, strict), jnp.logical_not(drop))
            bias_ref[...] = jnp.where(keep, 0.0, NEG_BIG)

        @pl.when(unresolved)
        def _exact():
            key = _sortable_key(sc)
            tk_ = _kth_key(lambda c: _row_count(key >= c), rows, k_top)
            eq = key == tk_
            need = k_top - (_row_count(key >= tk_) - _row_count(eq))
            keep_eq = jnp.logical_and(eq, jnp.logical_and(_dense_tie_ranks(eq) <= need, tk_ > INT_MIN))
            bias_ref[...] = jnp.where(jnp.logical_or(key > tk_, keep_eq), 0.0, NEG_BIG)


def _dsa_prompt_kernel(q_ref, k_ref, v_ref, qi_ref, ki_ref, pg_ref, o_ref, sc_ref, thr_ref, strict_ref,
                       *, tq, tk, k_top):
    i = pl.program_id(1)
    r0 = i * tq
    nkb = (r0 + tq + tk - 1) // tk
    row = r0 + lax.broadcasted_iota(I32, (tq, tk), 0)
    col0 = lax.broadcasted_iota(I32, (tq, tk), 1)

    qi = qi_ref[0].reshape(IDX_HEADS * tq, IDX_DIM)
    w = pg_ref[0][:, 2 * M_HEADS:2 * M_HEADS + IDX_HEADS] * (IDX_HEADS ** -0.5)

    def score_body(kb, carry):
        lo, hi = carry
        start = pl.multiple_of(kb * tk, tk)
        d = _dot_nt(qi, ki_ref[0, pl.ds(start, tk), :])
        d = jnp.maximum(d, 0.0).reshape(IDX_HEADS, tq, tk)
        sc = d[0] * w[:, 0:1]
        for h in range(1, IDX_HEADS):
            sc = sc + d[h] * w[:, h:h + 1]
        vis = col0 + kb * tk <= row
        sc_vis = jnp.where(vis, sc, -jnp.inf)
        sc_ref[kb] = sc_vis
        lo = jnp.minimum(lo, _fold_lanes_min(jnp.where(vis, sc, jnp.inf)))
        hi = jnp.maximum(hi, _fold_lanes_max(sc_vis))
        return lo, hi
    lo0, hi0 = lax.fori_loop(0, nkb, score_body,
                             (jnp.full((tq, LANES), jnp.inf, F32), jnp.full((tq, LANES), -jnp.inf, F32)))
    lo0 = jnp.min(lo0, axis=-1, keepdims=True)
    hi0 = jnp.max(hi0, axis=-1, keepdims=True)

    def count_blocks(pred):
        def body(kb, acc):
            return acc + _fold_lanes(jnp.where(pred(sc_ref[kb]), 1.0, 0.0))
        acc = lax.fori_loop(0, nkb, body, jnp.zeros((tq, LANES), F32))
        return jnp.sum(acc, axis=-1, keepdims=True)

    rep = lambda col: jnp.broadcast_to(col, (tq, LANES))

    def fold_tiles(tile_fn, init, combine, *reps):
        parts = []
        for r in range(0, tq, COUNT_ROWS):
            ops = [x[r:r + COUNT_ROWS] for x in reps]

            def body(kb, acc, r=r, ops=ops):
                for j in range(tk // LANES):
                    tile = sc_ref[kb, r:r + COUNT_ROWS, j * LANES:(j + 1) * LANES]
                    acc = combine(acc, tile_fn(tile, *ops))
                return acc
            parts.append(lax.fori_loop(0, nkb, body, jnp.full((COUNT_ROWS, LANES), init, F32)))
        return jnp.concatenate(parts, axis=0)

    ones_sq = jnp.ones((LANES, LANES), BF16)
    lane_sum = lambda part: _dot(part.astype(BF16), ones_sq)
    count_ge_rep = lambda t_rep: lane_sum(fold_tiles(lambda tile, t: jnp.where(tile >= t, 1.0, 0.0), 0.0, jnp.add, t_rep))

    def count_gt(v):
        return lane_sum(fold_tiles(lambda tile, t: jnp.where(tile > t, 1.0, 0.0), 0.0, jnp.add, rep(v)))[:, 0:1]

    def min_selected(thr, strict):
        at_least = rep(jnp.where(strict > 0.0, F32_LOWEST, thr))
        above = rep(jnp.where(strict > 0.0, thr, F32_LOWEST))
        pick = lambda tile, a, b: jnp.where(jnp.logical_and(tile >= a, tile > b), tile, jnp.inf)
        return jnp.min(fold_tiles(pick, jnp.inf, jnp.minimum, at_least, above), axis=-1, keepdims=True)

    lo = _bisect(count_ge_rep, rep(lo0), rep(hi0), k_top)[:, 0:1]
    c_lo = count_ge_rep(rep(lo))[:, 0:1]
    thr_ref[...] = rep(lo)
    strict_ref[...] = jnp.zeros((tq, LANES), F32)

    @pl.when(jnp.max(c_lo) > k_top)
    def _refine():
        thr, strict, c_gt, tie, unresolved = _walk_up(lo, c_lo, k_top, min_selected, count_gt)
        thr_ref[...] = jnp.broadcast_to(thr, (tq, LANES))
        strict_ref[...] = jnp.broadcast_to(strict, (tq, LANES))

        @pl.when(jnp.logical_and(jnp.logical_not(unresolved), jnp.max(tie) > 0.0))
        def _ties():
            def body(kb, before):
                sc = sc_ref[kb]
                eq = jnp.logical_and(sc == thr, tie > 0.0)
                drop = jnp.logical_and(eq, _tie_rank(eq, before) > k_top - c_gt)
                sc_ref[kb] = jnp.where(drop, -jnp.inf, sc)
                return before + _row_count(eq)
            lax.fori_loop(0, nkb, body, jnp.zeros((tq, 1), F32))

        @pl.when(unresolved)
        def _exact():
            tk_ = _kth_key(lambda c: count_blocks(lambda sc: _sortable_key(sc) >= c), tq, k_top)
            n_ge = count_blocks(lambda sc: _sortable_key(sc) >= tk_)
            n_eq = count_blocks(lambda sc: _sortable_key(sc) == tk_)
            need = k_top - (n_ge - n_eq)

            def body(kb, before):
                sc = sc_ref[kb]
                key = _sortable_key(sc)
                eq = key == tk_
                keep_eq = jnp.logical_and(eq, jnp.logical_and(_tie_rank(eq, before) <= need, tk_ > INT_MIN))
                sc_ref[kb] = jnp.where(jnp.logical_or(key > tk_, keep_eq), sc, -jnp.inf)
                return before + _row_count(eq)
            lax.fori_loop(0, nkb, body, jnp.zeros((tq, 1), F32))
            thr_ref[...] = jnp.full((tq, LANES), F32_LOWEST, F32)
            strict_ref[...] = jnp.zeros((tq, LANES), F32)

    thr = thr_ref[...][:, 0:1]
    strict = strict_ref[...][:, 0:1]
    rows2 = A_GROUP * tq

    lane = lax.broadcasted_iota(I32, (tq, LANES), 1)
    for g0 in range(0, A_KV_HEADS, ATT_HEADS_PER_LOOP):
        heads = range(g0, g0 + ATT_HEADS_PER_LOOP)

        def att_body(kb, carry, heads=heads):
            start = pl.multiple_of(kb * tk, tk)
            bias = jnp.where(_selected(sc_ref[kb], thr, strict), 0.0, NEG_BIG).astype(BF16)
            bias = jnp.concatenate([bias] * A_GROUP, axis=0)
            new = []
            for g, (m, acc) in zip(heads, carry):
                q2 = q_ref[0, A_GROUP * g:A_GROUP * (g + 1)].reshape(rows2, A_HEAD_DIM)
                s = _dot_nt(q2, k_ref[0, g, pl.ds(start, tk), :]).astype(BF16) + bias
                m_new = jnp.maximum(m, jnp.max(s, axis=-1, keepdims=True).astype(F32))
                p = jnp.exp2(s - m_new.astype(BF16))
                acc = jnp.exp2(m - m_new) * acc + _dot(p, v_ref[0, g, pl.ds(start, tk), :])
                new.append((m_new, acc))
            return tuple(new)
        init = tuple((jnp.full((rows2, 1), NEG_BIG, F32), jnp.zeros((rows2, LANES), F32)) for _ in heads)
        pairs = lax.fori_loop(0, nkb // 2, lambda i, c: att_body(2 * i + 1, att_body(2 * i, c)), init)
        final = lax.cond(nkb % 2 == 1, lambda c: att_body(nkb - 1, c), lambda c: c, pairs)
        for g, (_, acc) in zip(heads, final):
            o = acc / acc[:, A_HEAD_DIM:A_HEAD_DIM + 1]
            slab = jnp.where(lane < A_HEAD_DIM, o[0:tq], pltpu.roll(o[tq:2 * tq], A_HEAD_DIM, 1))
            o_ref[0, :, g * LANES:(g + 1) * LANES] = slab.astype(o_ref.dtype)


def _dsa_prompt(q_hm, k_hm, v_dup, qi_hm, ki_bf, pg):
    b, _, s, _ = q_hm.shape
    tq = min(DSA_TQ, s)
    tk = min(DSA_TK, s)
    k_top = min(TOPK_MAX, s // 4)
    qb = lambda i, j: (i, 0, j, 0)
    whole = lambda i, j: (i, 0, 0, 0)
    kern = functools.partial(_dsa_prompt_kernel, tq=tq, tk=tk, k_top=k_top)
    return pl.pallas_call(
        kern,
        grid=(b, s // tq),
        in_specs=[
            pl.BlockSpec((1, A_HEADS, tq, A_HEAD_DIM), qb),
            pl.BlockSpec((1, A_KV_HEADS, s, A_HEAD_DIM), whole),
            pl.BlockSpec((1, A_KV_HEADS, s, LANES), whole),
            pl.BlockSpec((1, IDX_HEADS, tq, IDX_DIM), qb),
            pl.BlockSpec((1, s, IDX_DIM), lambda i, j: (i, 0, 0)),
            pl.BlockSpec((1, tq, PG_W), lambda i, j: (i, j, 0)),
        ],
        out_specs=pl.BlockSpec((1, tq, A_WIDTH), lambda i, j: (i, j, 0)),
        out_shape=jax.ShapeDtypeStruct((b, s, A_WIDTH), BF16),
        scratch_shapes=[pltpu.VMEM((s // tk, tq, tk), F32), pltpu.VMEM((tq, LANES), F32),
                        pltpu.VMEM((tq, LANES), F32)],
        compiler_params=_cparams(("arbitrary", "arbitrary")),
        name="dsa_prompt",
    )(q_hm, k_hm, v_dup, qi_hm, ki_bf, pg)


def _page_copies(pt_ref, cidx_hbm, ck_hbm, cv_hbm, kidx_buf, k_buf, v_buf, sems, seq, slot, *, layer, n_pages):
    copies = []
    for p in range(n_pages):
        page = pt_ref[seq * n_pages + p]
        cols = pl.ds(p * PAGE_SIZE, PAGE_SIZE)
        copies.append(pltpu.make_async_copy(cidx_hbm.at[layer, page], kidx_buf.at[slot, :, cols], sems.at[slot, 0]))
        copies.append(pltpu.make_async_copy(ck_hbm.at[layer, page], k_buf.at[slot, :, :, cols], sems.at[slot, 1]))
        copies.append(pltpu.make_async_copy(cv_hbm.at[layer, page], v_buf.at[slot, :, :, cols], sems.at[slot, 2]))
    return copies


def _dsa_sample_kernel(pt_ref, q_ref, qi_ref, w_ref, knt_ref, vnt_ref, kint_ref, cidx_hbm, ck_hbm, cv_hbm,
                       o_ref, kidx_buf, k_buf, v_buf, bias_ref, sems, *, layer, n_pages, t_new, k_top):
    b = pl.program_id(0)
    slot = b % 2
    rows = SAMPLE_ROWS
    copies = functools.partial(_page_copies, pt_ref, cidx_hbm, ck_hbm, cv_hbm, kidx_buf, k_buf, v_buf, sems,
                               layer=layer, n_pages=n_pages)

    @pl.when(b == 0)
    def _first():
        for c in copies(0, 0):
            c.start()

    @pl.when(b + 1 < pl.num_programs(0))
    def _next():
        for c in copies(b + 1, 1 - slot):
            c.start()

    for c in copies(b, slot):
        c.wait()

    w = w_ref[0][:, 2 * M_HEADS:2 * M_HEADS + IDX_HEADS] * (IDX_HEADS ** -0.5)

    def scores(ki_t):
        d = jnp.maximum(_dot(qi_ref[0], ki_t), 0.0).reshape(IDX_HEADS, rows, ki_t.shape[1])
        sc = d[0] * w[:, 0:1]
        for h in range(1, IDX_HEADS):
            sc = sc + d[h] * w[:, h:h + 1]
        return sc

    t_i = lax.broadcasted_iota(I32, (rows, PAGE_SIZE), 0)
    j_i = lax.broadcasted_iota(I32, (rows, PAGE_SIZE), 1)
    vis = jnp.logical_and(j_i <= t_i, j_i < t_new)
    sc = jnp.concatenate([scores(kidx_buf[slot].astype(BF16)),
                          jnp.where(vis, scores(kint_ref[0]), -jnp.inf)], axis=-1)
    _select_dense(sc, k_top, bias_ref)
    bias = jnp.concatenate([bias_ref[...]] * A_GROUP, axis=0)
    for h in range(A_KV_HEADS):
        qh = q_ref[0, h]
        s = jnp.concatenate([_dot(qh, k_buf[slot, h].astype(BF16)), _dot(qh, knt_ref[0, h])], axis=-1) + bias
        pr = jnp.exp2(s - jnp.max(s, axis=-1, keepdims=True))
        prb = pr.astype(BF16)
        past = n_pages * PAGE_SIZE
        o = _dot_nt(prb[:, 0:past], v_buf[slot, h].astype(BF16)) + _dot_nt(prb[:, past:], vnt_ref[0, h])
        o = o / jnp.sum(pr, axis=-1, keepdims=True)
        o_ref[0, A_GROUP * h:A_GROUP * (h + 1)] = o.reshape(A_GROUP, rows, A_HEAD_DIM).astype(o_ref.dtype)


def _dsa_sample(q_g, qi_all, w8, kn_t, vn_t, kin_t, ck_t, cv_t, cidx_t, page_table, *, layer, t_new):
    db = q_g.shape[0]
    n_pages = page_table.shape[1]
    past = n_pages * PAGE_SIZE
    k_top = min(TOPK_MAX, (past + t_new) // 4)
    per_b3 = lambda b, pt: (b, 0, 0)
    per_b4 = lambda b, pt: (b, 0, 0, 0)
    hbm = pl.BlockSpec(memory_space=pl.ANY)
    grid_spec = pltpu.PrefetchScalarGridSpec(
        num_scalar_prefetch=1,
        grid=(db,),
        in_specs=[
            pl.BlockSpec((1, A_KV_HEADS, A_GROUP * SAMPLE_ROWS, A_HEAD_DIM), per_b4),
            pl.BlockSpec((1, IDX_HEADS * SAMPLE_ROWS, IDX_DIM), per_b3),
            pl.BlockSpec((1, SAMPLE_ROWS, PG_W), per_b3),
            pl.BlockSpec((1, A_KV_HEADS, A_HEAD_DIM, PAGE_SIZE), per_b4),
            pl.BlockSpec((1, A_KV_HEADS, A_HEAD_DIM, PAGE_SIZE), per_b4),
            pl.BlockSpec((1, IDX_DIM, PAGE_SIZE), per_b3),
            hbm, hbm, hbm,
        ],
        out_specs=pl.BlockSpec((1, A_HEADS, SAMPLE_ROWS, A_HEAD_DIM), per_b4),
        scratch_shapes=[
            pltpu.VMEM((2, IDX_DIM, past), F32),
            pltpu.VMEM((2, A_KV_HEADS, A_HEAD_DIM, past), F32),
            pltpu.VMEM((2, A_KV_HEADS, A_HEAD_DIM, past), F32),
            pltpu.VMEM((SAMPLE_ROWS, past + PAGE_SIZE), F32),
            pltpu.SemaphoreType.DMA((2, 3)),
        ],
    )
    kern = functools.partial(_dsa_sample_kernel, layer=layer, n_pages=n_pages, t_new=t_new, k_top=k_top)
    return pl.pallas_call(
        kern, grid_spec=grid_spec,
        out_shape=jax.ShapeDtypeStruct((db, A_HEADS, SAMPLE_ROWS, A_HEAD_DIM), BF16),
        compiler_params=_cparams(("arbitrary",)),
        name="dsa_sample",
    )(page_table.reshape(-1), q_g, qi_all, w8, kn_t, vn_t, kin_t, cidx_t, ck_t, cv_t)


def _sample_layouts(q_hm, qi_hm, k_o, v_o, ki_o, pg):
    db, _, t, _ = q_hm.shape
    pad_t = SAMPLE_ROWS - t
    pad_rows = lambda x: jnp.pad(x, ((0, 0), (0, 0), (0, pad_t), (0, 0)))
    q_g = pad_rows(q_hm).reshape(db, A_KV_HEADS, A_GROUP * SAMPLE_ROWS, A_HEAD_DIM)
    qi_all = pad_rows(qi_hm).reshape(db, IDX_HEADS * SAMPLE_ROWS, IDX_DIM)
    w8 = jnp.pad(pg, ((0, 0), (0, pad_t), (0, 0)))

    def heads_t(x):
        xt = x.reshape(db, t, A_KV_HEADS, A_HEAD_DIM).transpose(0, 2, 3, 1)
        return jnp.pad(xt, ((0, 0), (0, 0), (0, 0), (0, PAGE_SIZE - t))).astype(BF16)

    kin_t = jnp.pad(ki_o.transpose(0, 2, 1), ((0, 0), (0, 0), (0, PAGE_SIZE - t))).astype(BF16)
    return q_g, qi_all, w8, heads_t(k_o), heads_t(v_o), kin_t


def _finish_kernel(x_ref, hm_ref, a_ref, wo_ref, gf_ref, wgu_ref, wd_ref, y_ref):
    x1 = (x_ref[...] + _dot(hm_ref[...], wo_ref[0:M_WIDTH, :]) + _dot(a_ref[...], wo_ref[M_WIDTH:M_WIDTH + A_WIDTH, :]))
    ms = jnp.mean(x1 * x1, axis=-1, keepdims=True)
    xn = (x1 * lax.rsqrt(ms + NORM_EPS) * gf_ref[...]).astype(BF16)
    g = _dot(xn, wgu_ref[:, 0:D_FF])
    u = _dot(xn, wgu_ref[:, D_FF:2 * D_FF])
    y_ref[...] = x1 + _dot((g * jax.nn.sigmoid(g) * u).astype(BF16), wd_ref[...])


def _finish(x, hm, a, w_out, g_ffn, w_gu, w_down):
    n = x.shape[0]
    tm = min(ROW_TILE, n)
    const = lambda i: (0, 0)
    row = lambda i: (i, 0)
    return pl.pallas_call(
        _finish_kernel,
        grid=(n // tm,),
        in_specs=[
            pl.BlockSpec((tm, D_MODEL), row),
            pl.BlockSpec((tm, M_WIDTH), row),
            pl.BlockSpec((tm, A_WIDTH), row),
            pl.BlockSpec((M_WIDTH + A_WIDTH, D_MODEL), const),
            pl.BlockSpec((1, D_MODEL), const),
            pl.BlockSpec((D_MODEL, 2 * D_FF), const),
            pl.BlockSpec((D_FF, D_MODEL), const),
        ],
        out_specs=pl.BlockSpec((tm, D_MODEL), row),
        out_shape=jax.ShapeDtypeStruct((n, D_MODEL), F32),
        compiler_params=_cparams(("arbitrary",)),
        name="finish",
    )(x, hm, a, w_out, g_ffn, w_gu, w_down)


def _rope_tables(pos):
    half = A_HEAD_DIM // 2
    inv = ROPE_THETA ** (-jnp.arange(half, dtype=F32) / half)
    ang = pos.astype(F32)[:, None] * inv[None, :]
    cos, sin = jnp.cos(ang), jnp.sin(ang)
    cos_t = jnp.tile(jnp.concatenate([cos, cos], axis=-1), (1, LANES // A_HEAD_DIM))
    sin_t = jnp.tile(jnp.concatenate([-sin, sin], axis=-1), (1, LANES // A_HEAD_DIM))
    return cos_t, sin_t


def _layer_weights(l, g_mix, w_in, conv_w, b_gate, g_mout, g_q, g_k, w_out, g_ffn, w_gate_up, w_down):
    w = w_in[l]
    o_mi = PM_W
    o_aq = o_mi + 2 * M_HEADS
    o_iw = o_aq + PA_W + IDX_HEADS * IDX_DIM + IDX_DIM
    w_main = jnp.concatenate(
        [w[:, 0:PM_W], w[:, o_aq:o_iw], jnp.zeros((D_MODEL, LANES - IDX_DIM), F32)], axis=1).astype(BF16)
    w_gate = jnp.concatenate(
        [w[:, o_mi:o_aq], w[:, o_iw:o_iw + IDX_HEADS], jnp.zeros((D_MODEL, PG_W - 2 * M_HEADS - IDX_HEADS), F32)], axis=1)
    wg_hi, wg_lo = _split2(w_gate)
    bias_row = jnp.concatenate([b_gate[l], jnp.zeros((PG_W - 2 * M_HEADS,), F32)])[None, :]
    tile2 = lambda g: jnp.tile(g, LANES // A_HEAD_DIM)[None, :]
    return dict(
        g_mix=g_mix[l][None, :], w_main=w_main, wg_hi=wg_hi, wg_lo=wg_lo,
        conv_w=conv_w[l], bias_row=bias_row, g_mout=g_mout[l][None, :],
        gq=tile2(g_q[l]), gk=tile2(g_k[l]),
        w_out=w_out[l].astype(BF16), g_ffn=g_ffn[l][None, :],
        w_gu=w_gate_up[l].astype(BF16), w_down=w_down[l].astype(BF16))


def kernel(x_prompt, x_sample, cache_k, cache_v, cache_kidx, page_table, state_C, state_n, state_m, state_conv,
           g_mix, w_in, conv_w, b_gate, g_mout, g_q, g_k, w_out, g_ffn, w_gate_up, w_down):
    b, s, _ = x_prompt.shape
    db, t, _ = x_sample.shape
    depth = w_in.shape[0]
    past = page_table.shape[1] * PAGE_SIZE
    cos_p, sin_p = _rope_tables(jnp.arange(s, dtype=I32))
    cos_s, sin_s = _rope_tables(past + jnp.arange(t, dtype=I32))
    seg = (jnp.arange(LANES)[:, None] // A_HEAD_DIM == jnp.arange(LANES)[None, :] // A_HEAD_DIM).astype(BF16)
    ck_t = cache_k.transpose(0, 1, 3, 4, 2)
    cv_t = cache_v.transpose(0, 1, 3, 4, 2)
    cidx_t = cache_kidx.transpose(0, 1, 3, 2)
    xp = x_prompt.reshape(b * s, D_MODEL)
    xs = x_sample.reshape(db * t, D_MODEL)
    kp, vp, kip, cp, np_, mp, bp = [], [], [], [], [], [], []
    ks_, vs_, kis, cs, ns, ms, bs = [], [], [], [], [], [], []
    for l in range(depth):
        wl = _layer_weights(l, g_mix, w_in, conv_w, b_gate, g_mout, g_q, g_k, w_out, g_ffn, w_gate_up, w_down)
        pm, pa, pi, pg = _project(xp, wl["g_mix"], wl["w_main"], wl["wg_hi"], wl["wg_lo"])
        pm3, pg3 = pm.reshape(b, s, PM_W), pg.reshape(b, s, PG_W)
        hm, c_o, n_o, m_o, conv_o = _mlstm(pm3, pg3, wl["conv_w"], wl["bias_row"], wl["g_mout"], None)
        q_hm, k_hm, v_dup, qi_hm, ki_bf, k_o, v_o, ki_o = _aprep(
            pa.reshape(b, s, PA_W), pi.reshape(b, s, PI_W), cos_p, sin_p, wl["gq"], wl["gk"], seg)
        a = _dsa_prompt(q_hm, k_hm, v_dup, qi_hm, ki_bf, pg3)
        xp = _finish(xp, hm.reshape(b * s, M_WIDTH), a.reshape(b * s, A_WIDTH),
                     wl["w_out"], wl["g_ffn"], wl["w_gu"], wl["w_down"])
        kp.append(k_o.reshape(b, s, A_KV_HEADS, A_HEAD_DIM))
        vp.append(v_o.reshape(b, s, A_KV_HEADS, A_HEAD_DIM))
        kip.append(ki_o)
        cp.append(c_o)
        np_.append(n_o[:, :, 0, :])
        mp.append(m_o[:, :, 0, 0])
        bp.append(conv_o)

        pm, pa, pi, pg = _project(xs, wl["g_mix"], wl["w_main"], wl["wg_hi"], wl["wg_lo"])
        pg3 = pg.reshape(db, t, PG_W)
        state = (state_C[l], state_n[l][:, :, None, :],
                 jnp.broadcast_to(state_m[l][:, :, None, None], (db, M_HEADS, 1, LANES)), state_conv[l])
        hm, c_o, n_o, m_o, conv_o = _mlstm(pm.reshape(db, t, PM_W), pg3, wl["conv_w"], wl["bias_row"],
                                           wl["g_mout"], state)
        q_hm, _, _, qi_hm, _, k_o, v_o, ki_o = _aprep(
            pa.reshape(db, t, PA_W), pi.reshape(db, t, PI_W), cos_s, sin_s, wl["gq"], wl["gk"], seg)
        a = _dsa_sample(*_sample_layouts(q_hm, qi_hm, k_o, v_o, ki_o, pg3),
                        ck_t, cv_t, cidx_t, page_table, layer=l, t_new=t)
        a = a[:, :, 0:t, :].transpose(0, 2, 1, 3).reshape(db * t, A_WIDTH)
        xs = _finish(xs, hm.reshape(db * t, M_WIDTH), a, wl["w_out"], wl["g_ffn"], wl["w_gu"], wl["w_down"])
        ks_.append(k_o.reshape(db, t, A_KV_HEADS, A_HEAD_DIM))
        vs_.append(v_o.reshape(db, t, A_KV_HEADS, A_HEAD_DIM))
        kis.append(ki_o)
        cs.append(c_o)
        ns.append(n_o[:, :, 0, :])
        ms.append(m_o[:, :, 0, 0])
        bs.append(conv_o)
    return (xp.reshape(b, s, D_MODEL), xs.reshape(db, t, D_MODEL),
            jnp.stack(kp), jnp.stack(vp), jnp.stack(kip), jnp.stack(cp), jnp.stack(np_), jnp.stack(mp), jnp.stack(bp),
            jnp.stack(ks_), jnp.stack(vs_), jnp.stack(kis), jnp.stack(cs), jnp.stack(ns), jnp.stack(ms), jnp.stack(bs))
```

```python
import functools
import math

import jax
import jax.numpy as jnp
from jax import lax
from jax.experimental import pallas as pl
from jax.experimental.pallas import tpu as pltpu

F32 = jnp.float32
BF16 = jnp.bfloat16
I32 = jnp.int32

D_MODEL = 1024
M_HEADS = 4
M_HEAD_DIM = 128
M_WIDTH = M_HEADS * M_HEAD_DIM
CONV_WIDTH = 4
A_HEAD_DIM = 64
A_HEADS = 8
A_KV_HEADS = 4
A_GROUP = A_HEADS // A_KV_HEADS
A_WIDTH = A_HEADS * A_HEAD_DIM
A_KV_WIDTH = A_KV_HEADS * A_HEAD_DIM
IDX_HEADS = 8
IDX_DIM = 64
TOPK_MAX = 256
PAGE_SIZE = 128
ROPE_THETA = 10000.0
D_FF = 2816
NORM_EPS = 1e-6

LANES = 128
SUBLANES = 8
INT_MIN = -(2 ** 31)
NEG_BIG = -(2.0 ** 100)
F32_LOWEST = -3.0e38
LOG2E = math.log2(math.e)

PM_W = 4 * M_WIDTH
PA_W = A_WIDTH + 2 * A_KV_WIDTH
PI_W = IDX_HEADS * IDX_DIM + LANES
PG_W = LANES
MAIN_W = PM_W + PA_W + PI_W

ROW_TILE = 256
MLSTM_CHUNK = 256
DSA_TQ = 256
DSA_TK = 512
COUNT_ROWS = 128
ATT_HEADS_PER_LOOP = 4
SAMPLE_ROWS = SUBLANES
N_BISECT = 16
MAX_WALK = 8
VMEM_LIMIT = 56 * 1024 * 1024


def _cparams(sem):
    return pltpu.CompilerParams(dimension_semantics=sem, vmem_limit_bytes=VMEM_LIMIT)


def _split2(x):
    hi = x.astype(BF16)
    lo = (x - hi.astype(F32)).astype(BF16)
    return hi, lo


def _split3(x):
    a = x.astype(BF16)
    r = x - a.astype(F32)
    b = r.astype(BF16)
    c = (r - b.astype(F32)).astype(BF16)
    return a, b, c


def _dot(a, b):
    return jnp.dot(a, b, preferred_element_type=F32)


def _dot_nt(a, b):
    return lax.dot_general(a, b, (((1,), (1,)), ((), ())), preferred_element_type=F32)


def _proj_kernel(x_ref, g_ref, w_ref, wgh_ref, wgl_ref, pm_ref, pa_ref, pi_ref, pg_ref):
    x = x_ref[...]
    ms = jnp.mean(x * x, axis=-1, keepdims=True)
    y = x * lax.rsqrt(ms + NORM_EPS) * g_ref[...]
    yh, yl = _split2(y)
    pm_ref[...] = _dot(yh, w_ref[:, 0:PM_W])
    pa_ref[...] = _dot(yh, w_ref[:, PM_W:PM_W + PA_W])
    pi_ref[...] = _dot(yh, w_ref[:, PM_W + PA_W:MAIN_W])
    pg_ref[...] = _dot(yh, wgh_ref[...]) + _dot(yh, wgl_ref[...]) + _dot(yl, wgh_ref[...])


def _project(x, g, w_main, wg_hi, wg_lo):
    n = x.shape[0]
    tm = min(ROW_TILE, n)
    const = lambda i: (0, 0)
    return pl.pallas_call(
        _proj_kernel,
        grid=(n // tm,),
        in_specs=[
            pl.BlockSpec((tm, D_MODEL), lambda i: (i, 0)),
            pl.BlockSpec((1, D_MODEL), const),
            pl.BlockSpec((D_MODEL, MAIN_W), const),
            pl.BlockSpec((D_MODEL, PG_W), const),
            pl.BlockSpec((D_MODEL, PG_W), const),
        ],
        out_specs=[
            pl.BlockSpec((tm, PM_W), lambda i: (i, 0)),
            pl.BlockSpec((tm, PA_W), lambda i: (i, 0)),
            pl.BlockSpec((tm, PI_W), lambda i: (i, 0)),
            pl.BlockSpec((tm, PG_W), lambda i: (i, 0)),
        ],
        out_shape=[
            jax.ShapeDtypeStruct((n, PM_W), F32),
            jax.ShapeDtypeStruct((n, PA_W), F32),
            jax.ShapeDtypeStruct((n, PI_W), F32),
            jax.ShapeDtypeStruct((n, PG_W), F32),
        ],
        compiler_params=_cparams(("arbitrary",)),
        name="proj",
    )(x, g, w_main, wg_hi, wg_lo)


def _log_sigmoid(x):
    return jnp.minimum(x, 0.0) - jnp.log1p(jnp.exp(-jnp.abs(x)))


def _mlstm_kernel(*refs, lc, tb, tv, nc, has_state):
    if has_state:
        (pm_ref, pg_ref, cw_ref, bias_ref, gm_ref, c0_ref, n0_ref, m0_ref, conv0_ref,
         h_ref, c_out_ref, n_out_ref, m_out_ref, conv_out_ref,
         c_scr, n_scr, m_scr, ext_scr, u_scr) = refs
    else:
        (pm_ref, pg_ref, cw_ref, bias_ref, gm_ref,
         h_ref, c_out_ref, n_out_ref, m_out_ref, conv_out_ref,
         c_scr, n_scr, m_scr, ext_scr, u_scr) = refs
    c = pl.program_id(1)
    qk_w = 2 * M_WIDTH

    @pl.when(c == 0)
    def _init():
        ext_scr[0:SUBLANES, :] = jnp.zeros((SUBLANES, qk_w), F32)
        if has_state:
            c_scr[...] = c0_ref[0]
            n_scr[...] = n0_ref[0]
            m_scr[...] = m0_ref[0]
            ext_scr[SUBLANES - (CONV_WIDTH - 1):SUBLANES, :] = conv0_ref[0]
        else:
            c_scr[...] = jnp.zeros(c_scr.shape, F32)
            n_scr[...] = jnp.zeros(n_scr.shape, F32)
            m_scr[...] = jnp.zeros(m_scr.shape, F32)

    if tb < lc:
        u_scr[...] = jnp.zeros(u_scr.shape, F32)
        u_scr[0:tb, 0:PM_W] = pm_ref[0]
        u_scr[0:tb, PM_W:PM_W + PG_W] = pg_ref[0]
        pm = u_scr[:, 0:PM_W]
        gates = u_scr[:, PM_W:PM_W + PG_W]
    else:
        pm = pm_ref[0]
        gates = pg_ref[0]

    ext_scr[SUBLANES:SUBLANES + lc, :] = pm[:, 0:qk_w]
    cw = cw_ref[...]
    qk = ext_scr[pl.ds(SUBLANES - 3, lc), :] * cw[0:1, :]
    for j in range(1, CONV_WIDTH):
        qk = qk + ext_scr[pl.ds(SUBLANES - 3 + j, lc), :] * cw[j:j + 1, :]
    new_tail = ext_scr[pl.ds(tv, SUBLANES), :]
    qk = qk * jax.nn.sigmoid(qk)

    a = gates + bias_ref[...]
    ig = a
    lf = _log_sigmoid(a)
    if tv < lc:
        valid = lax.broadcasted_iota(I32, (lc, LANES), 0) < tv
        ig = jnp.where(valid, ig, NEG_BIG)
        lf = jnp.where(valid, lf, 0.0)
    ig_t = ig.T[0:SUBLANES, :]
    lf_t = lf.T[0:SUBLANES, :]

    r_i = lax.broadcasted_iota(I32, (lc, lc), 0)
    c_i = lax.broadcasted_iota(I32, (lc, lc), 1)
    causal = c_i <= r_i
    tril = jnp.where(causal, 1.0, 0.0).astype(BF16)
    triu = jnp.where(r_i <= c_i, 1.0, 0.0).astype(BF16)
    b_cols = sum(_dot(tril, p) for p in _split3(lf))
    b_rows = sum(_dot(p, triu) for p in _split3(lf_t))

    outs = []
    for h in range(M_HEADS):
        lo, hi = h * M_HEAD_DIM, (h + 1) * M_HEAD_DIM
        q = qk[:, lo:hi]
        k = qk[:, M_WIDTH + lo:M_WIDTH + hi] * (M_HEAD_DIM ** -0.5)
        v = pm[:, 2 * M_WIDTH + lo:2 * M_WIDTH + hi]
        og = pm[:, 3 * M_WIDTH + lo:3 * M_WIDTH + hi]
        b_col = b_cols[:, M_HEADS + h:M_HEADS + h + 1]
        i_col = ig[:, h:h + 1]
        b_row = b_rows[M_HEADS + h:M_HEADS + h + 1, :]
        i_row = ig_t[h:h + 1, :]
        m_prev = m_scr[h][:, 0:1]
        c_h = c_scr[h]
        n_h = n_scr[h]

        log_d = jnp.where(causal, b_col - b_row + i_row, NEG_BIG)
        inter = b_col + m_prev
        m_t = jnp.maximum(jnp.max(log_d, axis=-1, keepdims=True), inter)
        w_inter = jnp.exp(inter - m_t)
        d_mat = jnp.exp(log_d - m_t)
        qb = q.astype(BF16)
        s = _dot_nt(qb, k.astype(BF16)) * d_mat
        num = w_inter * _dot_nt(qb, c_h.astype(BF16)) + _dot(s.astype(BF16), v.astype(BF16))
        nq = w_inter * jnp.sum(q * n_h, axis=-1, keepdims=True) + jnp.sum(s, axis=-1, keepdims=True)
        hh = num / jnp.maximum(jnp.abs(nq), jnp.exp(-m_t))
        ms = jnp.mean(hh * hh, axis=-1, keepdims=True)
        hn = hh * lax.rsqrt(ms + NORM_EPS) * gm_ref[:, lo:hi]
        outs.append(jax.nn.sigmoid(og) * hn)

        b_last = b_cols[lc - 1:lc, M_HEADS + h:M_HEADS + h + 1]
        log_w = b_last - b_col + i_col
        m_new = jnp.maximum(b_last + m_prev, jnp.max(log_w, axis=0, keepdims=True))
        decay = jnp.exp(b_last + m_prev - m_new)
        kw = k * jnp.exp(log_w - m_new)
        c_scr[h] = decay * c_h + _dot(v.T.astype(BF16), kw.astype(BF16))
        n_scr[h] = decay * n_h + jnp.sum(kw, axis=0, keepdims=True)
        m_scr[h] = jnp.broadcast_to(m_new, (1, LANES))

    out = jnp.concatenate(outs, axis=-1)
    h_ref[0] = out[0:tb].astype(h_ref.dtype)
    ext_scr[0:SUBLANES, :] = new_tail

    @pl.when(c == nc - 1)
    def _fin():
        c_out_ref[0] = c_scr[...]
        n_out_ref[0] = n_scr[...]
        m_out_ref[0] = m_scr[...]
        conv_out_ref[0] = new_tail[SUBLANES - (CONV_WIDTH - 1):SUBLANES, :]


def _mlstm(pm, pg, conv_w, bias_row, g_mout, state):
    b, t, _ = pm.shape
    if t >= MLSTM_CHUNK:
        lc, tb, tv = MLSTM_CHUNK, MLSTM_CHUNK, MLSTM_CHUNK
    else:
        lc, tb, tv = LANES, t, t
    nc = max(t // lc, 1)
    has_state = state is not None
    qk_w = 2 * M_WIDTH
    const2 = lambda i, j: (0, 0)
    per_b4 = lambda i, j: (i, 0, 0, 0)
    in_specs = [
        pl.BlockSpec((1, tb, PM_W), lambda i, j: (i, j, 0)),
        pl.BlockSpec((1, tb, PG_W), lambda i, j: (i, j, 0)),
        pl.BlockSpec((CONV_WIDTH, qk_w), const2),
        pl.BlockSpec((1, PG_W), const2),
        pl.BlockSpec((1, M_WIDTH), const2),
    ]
    args = [pm, pg, conv_w, bias_row, g_mout]
    if has_state:
        in_specs += [
            pl.BlockSpec((1, M_HEADS, M_HEAD_DIM, M_HEAD_DIM), per_b4),
            pl.BlockSpec((1, M_HEADS, 1, M_HEAD_DIM), per_b4),
            pl.BlockSpec((1, M_HEADS, 1, LANES), per_b4),
            pl.BlockSpec((1, CONV_WIDTH - 1, qk_w), lambda i, j: (i, 0, 0)),
        ]
        args += list(state)
    out_specs = [
        pl.BlockSpec((1, tb, M_WIDTH), lambda i, j: (i, j, 0)),
        pl.BlockSpec((1, M_HEADS, M_HEAD_DIM, M_HEAD_DIM), per_b4),
        pl.BlockSpec((1, M_HEADS, 1, M_HEAD_DIM), per_b4),
        pl.BlockSpec((1, M_HEADS, 1, LANES), per_b4),
        pl.BlockSpec((1, CONV_WIDTH - 1, qk_w), lambda i, j: (i, 0, 0)),
    ]
    out_shape = [
        jax.ShapeDtypeStruct((b, t, M_WIDTH), BF16),
        jax.ShapeDtypeStruct((b, M_HEADS, M_HEAD_DIM, M_HEAD_DIM), F32),
        jax.ShapeDtypeStruct((b, M_HEADS, 1, M_HEAD_DIM), F32),
        jax.ShapeDtypeStruct((b, M_HEADS, 1, LANES), F32),
        jax.ShapeDtypeStruct((b, CONV_WIDTH - 1, qk_w), F32),
    ]
    scratch = [
        pltpu.VMEM((M_HEADS, M_HEAD_DIM, M_HEAD_DIM), F32),
        pltpu.VMEM((M_HEADS, 1, M_HEAD_DIM), F32),
        pltpu.VMEM((M_HEADS, 1, LANES), F32),
        pltpu.VMEM((lc + 2 * SUBLANES, qk_w), F32),
        pltpu.VMEM((lc, PM_W + PG_W), F32),
    ]
    kern = functools.partial(_mlstm_kernel, lc=lc, tb=tb, tv=tv, nc=nc, has_state=has_state)
    return pl.pallas_call(
        kern, grid=(b, nc), in_specs=in_specs, out_specs=out_specs, out_shape=out_shape,
        scratch_shapes=scratch, compiler_params=_cparams(("arbitrary", "arbitrary")), name="mlstm",
    )(*args)


def _swap_halves(x):
    lane = lax.broadcasted_iota(I32, x.shape, 1)
    first = (lane % A_HEAD_DIM) < (A_HEAD_DIM // 2)
    return jnp.where(first, pltpu.roll(x, LANES - A_HEAD_DIM // 2, 1), pltpu.roll(x, A_HEAD_DIM // 2, 1))


def _rope(x, cos, sin):
    return x * cos + _swap_halves(x) * sin


def _head_rms(x, seg, g):
    hi, lo = _split2(x * x)
    ss = _dot(hi, seg) + _dot(lo, seg)
    return x * lax.rsqrt(ss * (1.0 / A_HEAD_DIM) + NORM_EPS) * g


def _prep_heads(pa_blk, pi_blk, cos, sin, gq, gk, seg,
                q_ref, kh_ref, vd_ref, qi_ref, kib_ref, k_out_ref, v_out_ref, ki_out_ref):
    lane = lax.broadcasted_iota(I32, cos.shape, 1)
    low = lane < A_HEAD_DIM
    half = A_HEAD_DIM

    def split_heads(blk):
        return blk[:, 0:half], pltpu.roll(blk, half, 1)[:, 0:half]

    for j in range(A_WIDTH // LANES):
        blk = _rope(_head_rms(pa_blk(j * LANES), seg, gq), cos, sin)
        blk = blk * (A_HEAD_DIM ** -0.5 * LOG2E)
        h0, h1 = split_heads(blk)
        q_ref[0, 2 * j] = h0.astype(BF16)
        q_ref[0, 2 * j + 1] = h1.astype(BF16)
    for j in range(A_KV_WIDTH // LANES):
        blk = _rope(_head_rms(pa_blk(A_WIDTH + j * LANES), seg, gk), cos, sin)
        k_out_ref[0, :, j * LANES:(j + 1) * LANES] = blk
        h0, h1 = split_heads(blk)
        kh_ref[0, 2 * j] = h0.astype(BF16)
        kh_ref[0, 2 * j + 1] = h1.astype(BF16)
        vb = pa_blk(A_WIDTH + A_KV_WIDTH + j * LANES)
        v_out_ref[0, :, j * LANES:(j + 1) * LANES] = vb
        vr = pltpu.roll(vb, half, 1)
        ones_col = jnp.where(lane == half, 1.0, 0.0)
        vd_ref[0, 2 * j] = jnp.where(low, vb, ones_col).astype(BF16)
        vd_ref[0, 2 * j + 1] = jnp.where(low, vr, ones_col).astype(BF16)
    for j in range(IDX_HEADS * IDX_DIM // LANES):
        blk = _rope(pi_blk(j * LANES), cos, sin) * (IDX_DIM ** -0.5)
        h0, h1 = split_heads(blk)
        qi_ref[0, 2 * j] = h0.astype(BF16)
        qi_ref[0, 2 * j + 1] = h1.astype(BF16)
    kblk = _rope(pi_blk(IDX_HEADS * IDX_DIM), cos, sin)[:, 0:half]
    ki_out_ref[0] = kblk
    kib_ref[0] = kblk.astype(BF16)


def _aprep_kernel(pa_ref, pi_ref, cos_ref, sin_ref, gq_ref, gk_ref, seg_ref, *out_refs):
    _prep_heads(lambda c0: pa_ref[0, :, c0:c0 + LANES], lambda c0: pi_ref[0, :, c0:c0 + LANES],
                cos_ref[...], sin_ref[...], gq_ref[...], gk_ref[...], seg_ref[...], *out_refs)


def _proj_prep_kernel(x_ref, g_ref, w_ref, wgh_ref, wgl_ref, cos_ref, sin_ref, gq_ref, gk_ref, seg_ref,
                      pm_ref, pg_ref, *head_refs):
    x = x_ref[...]
    ms = jnp.mean(x * x, axis=-1, keepdims=True)
    y = x * lax.rsqrt(ms + NORM_EPS) * g_ref[...]
    yh, yl = _split2(y)
    pm_ref[...] = _dot(yh, w_ref[:, 0:PM_W])
    pg_ref[...] = _dot(yh, wgh_ref[...]) + _dot(yh, wgl_ref[...]) + _dot(yl, wgh_ref[...])
    pa = _dot(yh, w_ref[:, PM_W:PM_W + PA_W])
    pi = _dot(yh, w_ref[:, PM_W + PA_W:MAIN_W])
    _prep_heads(lambda c0: pa[:, c0:c0 + LANES], lambda c0: pi[:, c0:c0 + LANES],
                cos_ref[...], sin_ref[...], gq_ref[...], gk_ref[...], seg_ref[...], *head_refs)


def _proj_prep(x, seq_len, g, w_main, wg_hi, wg_lo, cos, sin, gq, gk, seg):
    n = x.shape[0]
    b = n // seq_len
    tm = min(ROW_TILE, seq_len)
    nsb = seq_len // tm
    const = lambda i: (0, 0)
    rows = lambda i: (i, 0)
    pos = lambda i: (i % nsb, 0)
    hm = lambda i: (i // nsb, 0, i % nsb, 0)
    row3 = lambda i: (i // nsb, i % nsb, 0)
    return pl.pallas_call(
        _proj_prep_kernel,
        grid=(n // tm,),
        in_specs=[
            pl.BlockSpec((tm, D_MODEL), rows),
            pl.BlockSpec((1, D_MODEL), const),
            pl.BlockSpec((D_MODEL, MAIN_W), const),
            pl.BlockSpec((D_MODEL, PG_W), const),
            pl.BlockSpec((D_MODEL, PG_W), const),
            pl.BlockSpec((tm, LANES), pos),
            pl.BlockSpec((tm, LANES), pos),
            pl.BlockSpec((1, LANES), const),
            pl.BlockSpec((1, LANES), const),
            pl.BlockSpec((LANES, LANES), const),
        ],
        out_specs=[
            pl.BlockSpec((tm, PM_W), rows),
            pl.BlockSpec((tm, PG_W), rows),
            pl.BlockSpec((1, A_HEADS, tm, A_HEAD_DIM), hm),
            pl.BlockSpec((1, A_KV_HEADS, tm, A_HEAD_DIM), hm),
            pl.BlockSpec((1, A_KV_HEADS, tm, LANES), hm),
            pl.BlockSpec((1, IDX_HEADS, tm, IDX_DIM), hm),
            pl.BlockSpec((1, tm, IDX_DIM), row3),
            pl.BlockSpec((1, tm, A_KV_WIDTH), row3),
            pl.BlockSpec((1, tm, A_KV_WIDTH), row3),
            pl.BlockSpec((1, tm, IDX_DIM), row3),
        ],
        out_shape=[
            jax.ShapeDtypeStruct((n, PM_W), F32),
            jax.ShapeDtypeStruct((n, PG_W), F32),
            jax.ShapeDtypeStruct((b, A_HEADS, seq_len, A_HEAD_DIM), BF16),
            jax.ShapeDtypeStruct((b, A_KV_HEADS, seq_len, A_HEAD_DIM), BF16),
            jax.ShapeDtypeStruct((b, A_KV_HEADS, seq_len, LANES), BF16),
            jax.ShapeDtypeStruct((b, IDX_HEADS, seq_len, IDX_DIM), BF16),
            jax.ShapeDtypeStruct((b, seq_len, IDX_DIM), BF16),
            jax.ShapeDtypeStruct((b, seq_len, A_KV_WIDTH), F32),
            jax.ShapeDtypeStruct((b, seq_len, A_KV_WIDTH), F32),
            jax.ShapeDtypeStruct((b, seq_len, IDX_DIM), F32),
        ],
        compiler_params=_cparams(("arbitrary",)),
        name="proj_prep",
    )(x, g, w_main, wg_hi, wg_lo, cos, sin, gq, gk, seg)


def _aprep(pa, pi, cos, sin, gq, gk, seg):
    b, t, _ = pa.shape
    tm = min(ROW_TILE, t)
    c2 = lambda i, j: (0, 0)
    row3 = lambda i, j: (i, j, 0)
    hm = lambda i, j: (i, 0, j, 0)
    return pl.pallas_call(
        _aprep_kernel,
        grid=(b, t // tm),
        in_specs=[
            pl.BlockSpec((1, tm, PA_W), row3),
            pl.BlockSpec((1, tm, PI_W), row3),
            pl.BlockSpec((tm, LANES), lambda i, j: (j, 0)),
            pl.BlockSpec((tm, LANES), lambda i, j: (j, 0)),
            pl.BlockSpec((1, LANES), c2),
            pl.BlockSpec((1, LANES), c2),
            pl.BlockSpec((LANES, LANES), c2),
        ],
        out_specs=[
            pl.BlockSpec((1, A_HEADS, tm, A_HEAD_DIM), hm),
            pl.BlockSpec((1, A_KV_HEADS, tm, A_HEAD_DIM), hm),
            pl.BlockSpec((1, A_KV_HEADS, tm, LANES), hm),
            pl.BlockSpec((1, IDX_HEADS, tm, IDX_DIM), hm),
            pl.BlockSpec((1, tm, IDX_DIM), row3),
            pl.BlockSpec((1, tm, A_KV_WIDTH), row3),
            pl.BlockSpec((1, tm, A_KV_WIDTH), row3),
            pl.BlockSpec((1, tm, IDX_DIM), row3),
        ],
        out_shape=[
            jax.ShapeDtypeStruct((b, A_HEADS, t, A_HEAD_DIM), BF16),
            jax.ShapeDtypeStruct((b, A_KV_HEADS, t, A_HEAD_DIM), BF16),
            jax.ShapeDtypeStruct((b, A_KV_HEADS, t, LANES), BF16),
            jax.ShapeDtypeStruct((b, IDX_HEADS, t, IDX_DIM), BF16),
            jax.ShapeDtypeStruct((b, t, IDX_DIM), BF16),
            jax.ShapeDtypeStruct((b, t, A_KV_WIDTH), F32),
            jax.ShapeDtypeStruct((b, t, A_KV_WIDTH), F32),
            jax.ShapeDtypeStruct((b, t, IDX_DIM), F32),
        ],
        compiler_params=_cparams(("arbitrary", "arbitrary")),
        name="aprep",
    )(pa, pi, cos, sin, gq, gk, seg)


def _sortable_key(score):
    bits = lax.bitcast_convert_type(score, I32)
    key = bits ^ ((bits >> 31) & 0x7FFFFFFF)
    key = jnp.where(key == -1, 0, key)
    return jnp.where(score == -jnp.inf, INT_MIN, key)


def _fold_lanes(x):
    acc = x[:, 0:LANES]
    for j in range(1, x.shape[1] // LANES):
        acc = acc + x[:, j * LANES:(j + 1) * LANES]
    return acc


def _fold_lanes_min(x):
    acc = x[:, 0:LANES]
    for j in range(1, x.shape[1] // LANES):
        acc = jnp.minimum(acc, x[:, j * LANES:(j + 1) * LANES])
    return acc


def _fold_lanes_max(x):
    acc = x[:, 0:LANES]
    for j in range(1, x.shape[1] // LANES):
        acc = jnp.maximum(acc, x[:, j * LANES:(j + 1) * LANES])
    return acc


def _row_count(mask):
    return jnp.sum(_fold_lanes(jnp.where(mask, 1.0, 0.0)), axis=-1, keepdims=True)


def _bisect(count_ge, lo, hi, k_top):
    def body(_, c):
        lo, hi = c
        mid = lo + (hi - lo) * 0.5
        ge = count_ge(mid) >= k_top
        return jnp.where(ge, mid, lo), jnp.where(ge, hi, mid)
    return lax.fori_loop(0, N_BISECT, body, (lo, hi))[0]


def _selected(sc, thr, strict):
    at_least = jnp.where(strict > 0.0, F32_LOWEST, thr)
    above = jnp.where(strict > 0.0, thr, F32_LOWEST)
    return jnp.logical_and(sc >= at_least, sc > above)


def _walk_up(lo, c_lo, k_top, min_selected, count_gt):
    def cond(c):
        return jnp.logical_and(c[5], c[6] < MAX_WALK)

    def body(c):
        thr, strict, c_sel, c_gt, tie, _, it = c
        active = jnp.logical_and(c_sel > k_top, tie == 0.0)
        v = min_selected(thr, strict)
        c_above = count_gt(v)
        adv = jnp.logical_and(active, c_above >= k_top)
        stop = jnp.logical_and(active, c_above < k_top)
        again = jnp.max(jnp.where(jnp.logical_and(adv, c_above > k_top), 1.0, 0.0)) > 0.0
        return (jnp.where(active, v, thr), jnp.where(adv, 1.0, jnp.where(stop, 0.0, strict)),
                jnp.where(adv, c_above, c_sel), jnp.where(stop, c_above, c_gt), jnp.where(stop, 1.0, tie),
                again, it + 1)

    zero = jnp.zeros_like(lo)
    thr, strict, _, c_gt, tie, pending, _ = lax.while_loop(
        cond, body, (lo, zero, c_lo, zero, zero, jnp.max(c_lo) > k_top, jnp.int32(0)))
    return thr, strict, c_gt, tie, pending


def _kth_key(count_ge_key, rows, k_top):
    def body(it, thr):
        cand = thr ^ lax.shift_left(jnp.int32(1), 31 - it)
        return jnp.where(count_ge_key(cand) >= k_top, cand, thr)
    return lax.fori_loop(0, 32, body, jnp.full((rows, 1), INT_MIN, I32))


def _tie_rank(eq, before):
    n = eq.shape[1]
    r_i = lax.broadcasted_iota(I32, (n, n), 0)
    c_i = lax.broadcasted_iota(I32, (n, n), 1)
    triu = jnp.where(r_i <= c_i, 1.0, 0.0).astype(BF16)
    return before + _dot(jnp.where(eq, 1.0, 0.0).astype(BF16), triu)


def _dense_tie_ranks(eq):
    before = jnp.zeros((eq.shape[0], 1), F32)
    ranks = []
    for j in range(eq.shape[1] // LANES):
        eq_j = eq[:, j * LANES:(j + 1) * LANES]
        ranks.append(_tie_rank(eq_j, before))
        before = before + _row_count(eq_j)
    return jnp.concatenate(ranks, axis=-1)


def _select_dense(sc, k_top, bias_ref):
    rows, n = sc.shape
    lo0 = jnp.min(jnp.where(sc == -jnp.inf, jnp.inf, sc), axis=-1, keepdims=True)
    hi0 = jnp.max(sc, axis=-1, keepdims=True)
    count_ge = lambda thr: _row_count(sc >= thr)
    lo = _bisect(count_ge, lo0, hi0, k_top)
    c_lo = count_ge(lo)
    bias_ref[...] = jnp.where(sc >= lo, 0.0, NEG_BIG)

    @pl.when(jnp.max(c_lo) > k_top)
    def _refine():
        min_selected = lambda thr, strict: jnp.min(
            _fold_lanes_min(jnp.where(_selected(sc, thr, strict), sc, jnp.inf)), axis=-1, keepdims=True)
        thr, strict, c_gt, tie, unresolved = _walk_up(lo, c_lo, k_top, min_selected, lambda v: _row_count(sc > v))

        @pl.when(jnp.logical_not(unresolved))
        def _ties():
            eq = jnp.logical_and(sc == thr, tie > 0.0)
            drop = jnp.logical_and(eq, _dense_tie_ranks(eq) > k_top - c_gt)
            keep = jnp.logical_and(_selected(sc, thr, strict), jnp.logical_not(drop))
            bias_ref[...] = jnp.where(keep, 0.0, NEG_BIG)

        @pl.when(unresolved)
        def _exact():
            key = _sortable_key(sc)
            tk_ = _kth_key(lambda c: _row_count(key >= c), rows, k_top)
            eq = key == tk_
            need = k_top - (_row_count(key >= tk_) - _row_count(eq))
            keep_eq = jnp.logical_and(eq, jnp.logical_and(_dense_tie_ranks(eq) <= need, tk_ > INT_MIN))
            bias_ref[...] = jnp.where(jnp.logical_or(key > tk_, keep_eq), 0.0, NEG_BIG)


def _dsa_prompt_kernel(q_ref, k_ref, v_ref, qi_ref, ki_ref, pg_ref, o_ref, sc_ref, thr_ref, strict_ref,
                       *, tq, tk, k_top):
    i = pl.program_id(1)
    r0 = i * tq
    nkb = (r0 + tq + tk - 1) // tk
    row = r0 + lax.broadcasted_iota(I32, (tq, tk), 0)
    col0 = lax.broadcasted_iota(I32, (tq, tk), 1)

    qi = qi_ref[0].reshape(IDX_HEADS * tq, IDX_DIM)
    w = pg_ref[0][:, 2 * M_HEADS:2 * M_HEADS + IDX_HEADS] * (IDX_HEADS ** -0.5)

    def score_body(kb, carry):
        lo, hi = carry
        start = pl.multiple_of(kb * tk, tk)
        d = _dot_nt(qi, ki_ref[0, pl.ds(start, tk), :])
        d = jnp.maximum(d, 0.0).reshape(IDX_HEADS, tq, tk)
        sc = d[0] * w[:, 0:1]
        for h in range(1, IDX_HEADS):
            sc = sc + d[h] * w[:, h:h + 1]
        vis = col0 + kb * tk <= row
        sc_vis = jnp.where(vis, sc, -jnp.inf)
        sc_ref[kb] = sc_vis
        lo = jnp.minimum(lo, _fold_lanes_min(jnp.where(vis, sc, jnp.inf)))
        hi = jnp.maximum(hi, _fold_lanes_max(sc_vis))
        return lo, hi
    bounds = (jnp.full((tq, LANES), jnp.inf, F32), jnp.full((tq, LANES), -jnp.inf, F32))
    bounds = lax.fori_loop(0, nkb // 2, lambda i, c: score_body(2 * i + 1, score_body(2 * i, c)), bounds)
    lo0, hi0 = lax.cond(nkb % 2 == 1, lambda c: score_body(nkb - 1, c), lambda c: c, bounds)
    lo0 = jnp.min(lo0, axis=-1, keepdims=True)
    hi0 = jnp.max(hi0, axis=-1, keepdims=True)

    def count_blocks(pred):
        def body(kb, acc):
            return acc + _fold_lanes(jnp.where(pred(sc_ref[kb]), 1.0, 0.0))
        acc = lax.fori_loop(0, nkb, body, jnp.zeros((tq, LANES), F32))
        return jnp.sum(acc, axis=-1, keepdims=True)

    rep = lambda col: jnp.broadcast_to(col, (tq, LANES))

    def fold_tiles(tile_fn, init, combine, *reps):
        parts = []
        for r in range(0, tq, COUNT_ROWS):
            ops = [x[r:r + COUNT_ROWS] for x in reps]

            def body(kb, acc, r=r, ops=ops):
                for j in range(tk // LANES):
                    tile = sc_ref[kb, r:r + COUNT_ROWS, j * LANES:(j + 1) * LANES]
                    acc = combine(acc, tile_fn(tile, *ops))
                return acc
            parts.append(lax.fori_loop(0, nkb, body, jnp.full((COUNT_ROWS, LANES), init, F32)))
        return jnp.concatenate(parts, axis=0)

    ones_sq = jnp.ones((LANES, LANES), BF16)
    lane_sum = lambda part: _dot(part.astype(BF16), ones_sq)
    count_ge_rep = lambda t_rep: lane_sum(fold_tiles(lambda tile, t: jnp.where(tile >= t, 1.0, 0.0), 0.0, jnp.add, t_rep))

    def count_gt(v):
        return lane_sum(fold_tiles(lambda tile, t: jnp.where(tile > t, 1.0, 0.0), 0.0, jnp.add, rep(v)))[:, 0:1]

    def min_selected(thr, strict):
        at_least = rep(jnp.where(strict > 0.0, F32_LOWEST, thr))
        above = rep(jnp.where(strict > 0.0, thr, F32_LOWEST))
        pick = lambda tile, a, b: jnp.where(jnp.logical_and(tile >= a, tile > b), tile, jnp.inf)
        return jnp.min(fold_tiles(pick, jnp.inf, jnp.minimum, at_least, above), axis=-1, keepdims=True)

    lo = _bisect(count_ge_rep, rep(lo0), rep(hi0), k_top)[:, 0:1]
    c_lo = count_ge_rep(rep(lo))[:, 0:1]
    thr_ref[...] = rep(lo)
    strict_ref[...] = jnp.zeros((tq, LANES), F32)

    @pl.when(jnp.max(c_lo) > k_top)
    def _refine():
        thr, strict, c_gt, tie, unresolved = _walk_up(lo, c_lo, k_top, min_selected, count_gt)
        thr_ref[...] = jnp.broadcast_to(thr, (tq, LANES))
        strict_ref[...] = jnp.broadcast_to(strict, (tq, LANES))

        @pl.when(jnp.logical_and(jnp.logical_not(unresolved), jnp.max(tie) > 0.0))
        def _ties():
            def body(kb, before):
                sc = sc_ref[kb]
                eq = jnp.logical_and(sc == thr, tie > 0.0)
                drop = jnp.logical_and(eq, _tie_rank(eq, before) > k_top - c_gt)
                sc_ref[kb] = jnp.where(drop, -jnp.inf, sc)
                return before + _row_count(eq)
            lax.fori_loop(0, nkb, body, jnp.zeros((tq, 1), F32))

        @pl.when(unresolved)
        def _exact():
            tk_ = _kth_key(lambda c: count_blocks(lambda sc: _sortable_key(sc) >= c), tq, k_top)
            n_ge = count_blocks(lambda sc: _sortable_key(sc) >= tk_)
            n_eq = count_blocks(lambda sc: _sortable_key(sc) == tk_)
            need = k_top - (n_ge - n_eq)

            def body(kb, before):
                sc = sc_ref[kb]
                key = _sortable_key(sc)
                eq = key == tk_
                keep_eq = jnp.logical_and(eq, jnp.logical_and(_tie_rank(eq, before) <= need, tk_ > INT_MIN))
                sc_ref[kb] = jnp.where(jnp.logical_or(key > tk_, keep_eq), sc, -jnp.inf)
                return before + _row_count(eq)
            lax.fori_loop(0, nkb, body, jnp.zeros((tq, 1), F32))
            thr_ref[...] = jnp.full((tq, LANES), F32_LOWEST, F32)
            strict_ref[...] = jnp.zeros((tq, LANES), F32)

    thr = thr_ref[...][:, 0:1]
    strict = strict_ref[...][:, 0:1]
    rows2 = A_GROUP * tq

    lane = lax.broadcasted_iota(I32, (tq, LANES), 1)
    for g0 in range(0, A_KV_HEADS, ATT_HEADS_PER_LOOP):
        heads = range(g0, g0 + ATT_HEADS_PER_LOOP)

        def att_body(kb, carry, heads=heads):
            start = pl.multiple_of(kb * tk, tk)
            bias = jnp.where(_selected(sc_ref[kb], thr, strict), 0.0, NEG_BIG).astype(BF16)
            bias = jnp.concatenate([bias] * A_GROUP, axis=0)
            new = []
            for g, (m, acc) in zip(heads, carry):
                q2 = q_ref[0, A_GROUP * g:A_GROUP * (g + 1)].reshape(rows2, A_HEAD_DIM)
                s = _dot_nt(q2, k_ref[0, g, pl.ds(start, tk), :]).astype(BF16) + bias
                m_new = jnp.maximum(m, jnp.max(s, axis=-1, keepdims=True).astype(F32))
                p = jnp.exp2(s - m_new.astype(BF16))
                acc = jnp.exp2(m - m_new) * acc + _dot(p, v_ref[0, g, pl.ds(start, tk), :])
                new.append((m_new, acc))
            return tuple(new)
        init = tuple((jnp.full((rows2, 1), NEG_BIG, F32), jnp.zeros((rows2, LANES), F32)) for _ in heads)
        pairs = lax.fori_loop(0, nkb // 2, lambda i, c: att_body(2 * i + 1, att_body(2 * i, c)), init)
        final = lax.cond(nkb % 2 == 1, lambda c: att_body(nkb - 1, c), lambda c: c, pairs)
        for g, (_, acc) in zip(heads, final):
            o = acc / acc[:, A_HEAD_DIM:A_HEAD_DIM + 1]
            slab = jnp.where(lane < A_HEAD_DIM, o[0:tq], pltpu.roll(o[tq:2 * tq], A_HEAD_DIM, 1))
            o_ref[0, :, g * LANES:(g + 1) * LANES] = slab.astype(o_ref.dtype)


def _dsa_prompt(q_hm, k_hm, v_dup, qi_hm, ki_bf, pg):
    b, _, s, _ = q_hm.shape
    tq = min(DSA_TQ, s)
    tk = min(DSA_TK, s)
    k_top = min(TOPK_MAX, s // 4)
    qb = lambda i, j: (i, 0, j, 0)
    whole = lambda i, j: (i, 0, 0, 0)
    kern = functools.partial(_dsa_prompt_kernel, tq=tq, tk=tk, k_top=k_top)
    return pl.pallas_call(
        kern,
        grid=(b, s // tq),
        in_specs=[
            pl.BlockSpec((1, A_HEADS, tq, A_HEAD_DIM), qb),
            pl.BlockSpec((1, A_KV_HEADS, s, A_HEAD_DIM), whole),
            pl.BlockSpec((1, A_KV_HEADS, s, LANES), whole),
            pl.BlockSpec((1, IDX_HEADS, tq, IDX_DIM), qb),
            pl.BlockSpec((1, s, IDX_DIM), lambda i, j: (i, 0, 0)),
            pl.BlockSpec((1, tq, PG_W), lambda i, j: (i, j, 0)),
        ],
        out_specs=pl.BlockSpec((1, tq, A_WIDTH), lambda i, j: (i, j, 0)),
        out_shape=jax.ShapeDtypeStruct((b, s, A_WIDTH), BF16),
        scratch_shapes=[pltpu.VMEM((s // tk, tq, tk), F32), pltpu.VMEM((tq, LANES), F32),
                        pltpu.VMEM((tq, LANES), F32)],
        compiler_params=_cparams(("arbitrary", "arbitrary")),
        name="dsa_prompt",
    )(q_hm, k_hm, v_dup, qi_hm, ki_bf, pg)


def _page_copies(pt_ref, cidx_hbm, ck_hbm, cv_hbm, kidx_buf, k_buf, v_buf, sems, seq, slot, *, layer, n_pages):
    copies = []
    for p in range(n_pages):
        page = pt_ref[seq * n_pages + p]
        cols = pl.ds(p * PAGE_SIZE, PAGE_SIZE)
        copies.append(pltpu.make_async_copy(cidx_hbm.at[layer, page], kidx_buf.at[slot, :, cols], sems.at[slot, 0]))
        copies.append(pltpu.make_async_copy(ck_hbm.at[layer, page], k_buf.at[slot, :, :, cols], sems.at[slot, 1]))
        copies.append(pltpu.make_async_copy(cv_hbm.at[layer, page], v_buf.at[slot, :, :, cols], sems.at[slot, 2]))
    return copies


def _dsa_sample_kernel(pt_ref, q_ref, qi_ref, w_ref, knt_ref, vnt_ref, kint_ref, cidx_hbm, ck_hbm, cv_hbm,
                       o_ref, kidx_buf, k_buf, v_buf, bias_ref, sems, *, layer, n_pages, t_new, k_top):
    b = pl.program_id(0)
    slot = b % 2
    rows = SAMPLE_ROWS
    copies = functools.partial(_page_copies, pt_ref, cidx_hbm, ck_hbm, cv_hbm, kidx_buf, k_buf, v_buf, sems,
                               layer=layer, n_pages=n_pages)

    @pl.when(b == 0)
    def _first():
        for c in copies(0, 0):
            c.start()

    @pl.when(b + 1 < pl.num_programs(0))
    def _next():
        for c in copies(b + 1, 1 - slot):
            c.start()

    for c in copies(b, slot):
        c.wait()

    w = w_ref[0][:, 2 * M_HEADS:2 * M_HEADS + IDX_HEADS] * (IDX_HEADS ** -0.5)

    def scores(ki_t):
        d = jnp.maximum(_dot(qi_ref[0], ki_t), 0.0).reshape(IDX_HEADS, rows, ki_t.shape[1])
        sc = d[0] * w[:, 0:1]
        for h in range(1, IDX_HEADS):
            sc = sc + d[h] * w[:, h:h + 1]
        return sc

    t_i = lax.broadcasted_iota(I32, (rows, PAGE_SIZE), 0)
    j_i = lax.broadcasted_iota(I32, (rows, PAGE_SIZE), 1)
    vis = jnp.logical_and(j_i <= t_i, j_i < t_new)
    sc = jnp.concatenate([scores(kidx_buf[slot].astype(BF16)),
                          jnp.where(vis, scores(kint_ref[0]), -jnp.inf)], axis=-1)
    _select_dense(sc, k_top, bias_ref)
    bias = jnp.concatenate([bias_ref[...]] * A_GROUP, axis=0)
    for h in range(A_KV_HEADS):
        qh = q_ref[0, h]
        s = jnp.concatenate([_dot(qh, k_buf[slot, h].astype(BF16)), _dot(qh, knt_ref[0, h])], axis=-1) + bias
        pr = jnp.exp2(s - jnp.max(s, axis=-1, keepdims=True))
        prb = pr.astype(BF16)
        past = n_pages * PAGE_SIZE
        o = _dot_nt(prb[:, 0:past], v_buf[slot, h].astype(BF16)) + _dot_nt(prb[:, past:], vnt_ref[0, h])
        o = o / jnp.sum(pr, axis=-1, keepdims=True)
        o_ref[0, A_GROUP * h:A_GROUP * (h + 1)] = o.reshape(A_GROUP, rows, A_HEAD_DIM).astype(o_ref.dtype)


def _dsa_sample(q_g, qi_all, w8, kn_t, vn_t, kin_t, ck_t, cv_t, cidx_t, page_table, *, layer, t_new):
    db = q_g.shape[0]
    n_pages = page_table.shape[1]
    past = n_pages * PAGE_SIZE
    k_top = min(TOPK_MAX, (past + t_new) // 4)
    per_b3 = lambda b, pt: (b, 0, 0)
    per_b4 = lambda b, pt: (b, 0, 0, 0)
    hbm = pl.BlockSpec(memory_space=pl.ANY)
    grid_spec = pltpu.PrefetchScalarGridSpec(
        num_scalar_prefetch=1,
        grid=(db,),
        in_specs=[
            pl.BlockSpec((1, A_KV_HEADS, A_GROUP * SAMPLE_ROWS, A_HEAD_DIM), per_b4),
            pl.BlockSpec((1, IDX_HEADS * SAMPLE_ROWS, IDX_DIM), per_b3),
            pl.BlockSpec((1, SAMPLE_ROWS, PG_W), per_b3),
            pl.BlockSpec((1, A_KV_HEADS, A_HEAD_DIM, PAGE_SIZE), per_b4),
            pl.BlockSpec((1, A_KV_HEADS, A_HEAD_DIM, PAGE_SIZE), per_b4),
            pl.BlockSpec((1, IDX_DIM, PAGE_SIZE), per_b3),
            hbm, hbm, hbm,
        ],
        out_specs=pl.BlockSpec((1, A_HEADS, SAMPLE_ROWS, A_HEAD_DIM), per_b4),
        scratch_shapes=[
            pltpu.VMEM((2, IDX_DIM, past), F32),
            pltpu.VMEM((2, A_KV_HEADS, A_HEAD_DIM, past), F32),
            pltpu.VMEM((2, A_KV_HEADS, A_HEAD_DIM, past), F32),
            pltpu.VMEM((SAMPLE_ROWS, past + PAGE_SIZE), F32),
            pltpu.SemaphoreType.DMA((2, 3)),
        ],
    )
    kern = functools.partial(_dsa_sample_kernel, layer=layer, n_pages=n_pages, t_new=t_new, k_top=k_top)
    return pl.pallas_call(
        kern, grid_spec=grid_spec,
        out_shape=jax.ShapeDtypeStruct((db, A_HEADS, SAMPLE_ROWS, A_HEAD_DIM), BF16),
        compiler_params=_cparams(("arbitrary",)),
        name="dsa_sample",
    )(page_table.reshape(-1), q_g, qi_all, w8, kn_t, vn_t, kin_t, cidx_t, ck_t, cv_t)


def _sample_layouts(q_hm, qi_hm, k_o, v_o, ki_o, pg):
    db, _, t, _ = q_hm.shape
    pad_t = SAMPLE_ROWS - t
    pad_rows = lambda x: jnp.pad(x, ((0, 0), (0, 0), (0, pad_t), (0, 0)))
    q_g = pad_rows(q_hm).reshape(db, A_KV_HEADS, A_GROUP * SAMPLE_ROWS, A_HEAD_DIM)
    qi_all = pad_rows(qi_hm).reshape(db, IDX_HEADS * SAMPLE_ROWS, IDX_DIM)
    w8 = jnp.pad(pg, ((0, 0), (0, pad_t), (0, 0)))

    def heads_t(x):
        xt = x.reshape(db, t, A_KV_HEADS, A_HEAD_DIM).transpose(0, 2, 3, 1)
        return jnp.pad(xt, ((0, 0), (0, 0), (0, 0), (0, PAGE_SIZE - t))).astype(BF16)

    kin_t = jnp.pad(ki_o.transpose(0, 2, 1), ((0, 0), (0, 0), (0, PAGE_SIZE - t))).astype(BF16)
    return q_g, qi_all, w8, heads_t(k_o), heads_t(v_o), kin_t


def _finish_kernel(x_ref, hm_ref, a_ref, wo_ref, gf_ref, wgu_ref, wd_ref, y_ref):
    x1 = (x_ref[...] + _dot(hm_ref[...], wo_ref[0:M_WIDTH, :]) + _dot(a_ref[...], wo_ref[M_WIDTH:M_WIDTH + A_WIDTH, :]))
    ms = jnp.mean(x1 * x1, axis=-1, keepdims=True)
    xn = (x1 * lax.rsqrt(ms + NORM_EPS) * gf_ref[...]).astype(BF16)
    g = _dot(xn, wgu_ref[:, 0:D_FF])
    u = _dot(xn, wgu_ref[:, D_FF:2 * D_FF])
    y_ref[...] = x1 + _dot((g * jax.nn.sigmoid(g) * u).astype(BF16), wd_ref[...])


def _finish(x, hm, a, w_out, g_ffn, w_gu, w_down):
    n = x.shape[0]
    tm = min(ROW_TILE, n)
    const = lambda i: (0, 0)
    row = lambda i: (i, 0)
    return pl.pallas_call(
        _finish_kernel,
        grid=(n // tm,),
        in_specs=[
            pl.BlockSpec((tm, D_MODEL), row),
            pl.BlockSpec((tm, M_WIDTH), row),
            pl.BlockSpec((tm, A_WIDTH), row),
            pl.BlockSpec((M_WIDTH + A_WIDTH, D_MODEL), const),
            pl.BlockSpec((1, D_MODEL), const),
            pl.BlockSpec((D_MODEL, 2 * D_FF), const),
            pl.BlockSpec((D_FF, D_MODEL), const),
        ],
        out_specs=pl.BlockSpec((tm, D_MODEL), row),
        out_shape=jax.ShapeDtypeStruct((n, D_MODEL), F32),
        compiler_params=_cparams(("arbitrary",)),
        name="finish",
    )(x, hm, a, w_out, g_ffn, w_gu, w_down)


def _rope_tables(pos):
    half = A_HEAD_DIM // 2
    inv = ROPE_THETA ** (-jnp.arange(half, dtype=F32) / half)
    ang = pos.astype(F32)[:, None] * inv[None, :]
    cos, sin = jnp.cos(ang), jnp.sin(ang)
    cos_t = jnp.tile(jnp.concatenate([cos, cos], axis=-1), (1, LANES // A_HEAD_DIM))
    sin_t = jnp.tile(jnp.concatenate([-sin, sin], axis=-1), (1, LANES // A_HEAD_DIM))
    return cos_t, sin_t


def _layer_weights(l, g_mix, w_in, conv_w, b_gate, g_mout, g_q, g_k, w_out, g_ffn, w_gate_up, w_down):
    w = w_in[l]
    o_mi = PM_W
    o_aq = o_mi + 2 * M_HEADS
    o_iw = o_aq + PA_W + IDX_HEADS * IDX_DIM + IDX_DIM
    w_main = jnp.concatenate(
        [w[:, 0:PM_W], w[:, o_aq:o_iw], jnp.zeros((D_MODEL, LANES - IDX_DIM), F32)], axis=1).astype(BF16)
    w_gate = jnp.concatenate(
        [w[:, o_mi:o_aq], w[:, o_iw:o_iw + IDX_HEADS], jnp.zeros((D_MODEL, PG_W - 2 * M_HEADS - IDX_HEADS), F32)], axis=1)
    wg_hi, wg_lo = _split2(w_gate)
    bias_row = jnp.concatenate([b_gate[l], jnp.zeros((PG_W - 2 * M_HEADS,), F32)])[None, :]
    tile2 = lambda g: jnp.tile(g, LANES // A_HEAD_DIM)[None, :]
    return dict(
        g_mix=g_mix[l][None, :], w_main=w_main, wg_hi=wg_hi, wg_lo=wg_lo,
        conv_w=conv_w[l], bias_row=bias_row, g_mout=g_mout[l][None, :],
        gq=tile2(g_q[l]), gk=tile2(g_k[l]),
        w_out=w_out[l].astype(BF16), g_ffn=g_ffn[l][None, :],
        w_gu=w_gate_up[l].astype(BF16), w_down=w_down[l].astype(BF16))


def kernel(x_prompt, x_sample, cache_k, cache_v, cache_kidx, page_table, state_C, state_n, state_m, state_conv,
           g_mix, w_in, conv_w, b_gate, g_mout, g_q, g_k, w_out, g_ffn, w_gate_up, w_down):
    b, s, _ = x_prompt.shape
    db, t, _ = x_sample.shape
    depth = w_in.shape[0]
    past = page_table.shape[1] * PAGE_SIZE
    cos_p, sin_p = _rope_tables(jnp.arange(s, dtype=I32))
    cos_s, sin_s = _rope_tables(past + jnp.arange(t, dtype=I32))
    seg = (jnp.arange(LANES)[:, None] // A_HEAD_DIM == jnp.arange(LANES)[None, :] // A_HEAD_DIM).astype(BF16)
    ck_t = cache_k.transpose(0, 1, 3, 4, 2)
    cv_t = cache_v.transpose(0, 1, 3, 4, 2)
    cidx_t = cache_kidx.transpose(0, 1, 3, 2)
    xp = x_prompt.reshape(b * s, D_MODEL)
    xs = x_sample.reshape(db * t, D_MODEL)
    kp, vp, kip, cp, np_, mp, bp = [], [], [], [], [], [], []
    ks_, vs_, kis, cs, ns, ms, bs = [], [], [], [], [], [], []
    for l in range(depth):
        wl = _layer_weights(l, g_mix, w_in, conv_w, b_gate, g_mout, g_q, g_k, w_out, g_ffn, w_gate_up, w_down)
        pm, pg, q_hm, k_hm, v_ext, qi_hm, ki_bf, k_o, v_o, ki_o = _proj_prep(
            xp, s, wl["g_mix"], wl["w_main"], wl["wg_hi"], wl["wg_lo"], cos_p, sin_p, wl["gq"], wl["gk"], seg)
        pm3, pg3 = pm.reshape(b, s, PM_W), pg.reshape(b, s, PG_W)
        hm, c_o, n_o, m_o, conv_o = _mlstm(pm3, pg3, wl["conv_w"], wl["bias_row"], wl["g_mout"], None)
        a = _dsa_prompt(q_hm, k_hm, v_ext, qi_hm, ki_bf, pg3)
        xp = _finish(xp, hm.reshape(b * s, M_WIDTH), a.reshape(b * s, A_WIDTH),
                     wl["w_out"], wl["g_ffn"], wl["w_gu"], wl["w_down"])
        kp.append(k_o.reshape(b, s, A_KV_HEADS, A_HEAD_DIM))
        vp.append(v_o.reshape(b, s, A_KV_HEADS, A_HEAD_DIM))
        kip.append(ki_o)
        cp.append(c_o)
        np_.append(n_o[:, :, 0, :])
        mp.append(m_o[:, :, 0, 0])
        bp.append(conv_o)

        pm, pa, pi, pg = _project(xs, wl["g_mix"], wl["w_main"], wl["wg_hi"], wl["wg_lo"])
        pg3 = pg.reshape(db, t, PG_W)
        state = (state_C[l], state_n[l][:, :, None, :],
                 jnp.broadcast_to(state_m[l][:, :, None, None], (db, M_HEADS, 1, LANES)), state_conv[l])
        hm, c_o, n_o, m_o, conv_o = _mlstm(pm.reshape(db, t, PM_W), pg3, wl["conv_w"], wl["bias_row"],
                                           wl["g_mout"], state)
        q_hm, _, _, qi_hm, _, k_o, v_o, ki_o = _aprep(
            pa.reshape(db, t, PA_W), pi.reshape(db, t, PI_W), cos_s, sin_s, wl["gq"], wl["gk"], seg)
        a = _dsa_sample(*_sample_layouts(q_hm, qi_hm, k_o, v_o, ki_o, pg3),
                        ck_t, cv_t, cidx_t, page_table, layer=l, t_new=t)
        a = a[:, :, 0:t, :].transpose(0, 2, 1, 3).reshape(db * t, A_WIDTH)
        xs = _finish(xs, hm.reshape(db * t, M_WIDTH), a, wl["w_out"], wl["g_ffn"], wl["w_gu"], wl["w_down"])
        ks_.append(k_o.reshape(db, t, A_KV_HEADS, A_HEAD_DIM))
        vs_.append(v_o.reshape(db, t, A_KV_HEADS, A_HEAD_DIM))
        kis.append(ki_o)
        cs.append(c_o)
        ns.append(n_o[:, :, 0, :])
        ms.append(m_o[:, :, 0, 0])
        bs.append(conv_o)
    return (xp.reshape(b, s, D_MODEL), xs.reshape(db, t, D_MODEL),
            jnp.stack(kp), jnp.stack(vp), jnp.stack(kip), jnp.stack(cp), jnp.stack(np_), jnp.stack(mp), jnp.stack(bp),
            jnp.stack(ks_), jnp.stack(vs_), jnp.stack(kis), jnp.stack(cs), jnp.stack(ns), jnp.stack(ms), jnp.stack(bs))
```

```python
import functools
import math

import jax
import jax.numpy as jnp
from jax import lax
from jax.experimental import pallas as pl
from jax.experimental.pallas import tpu as pltpu

F32 = jnp.float32
BF16 = jnp.bfloat16
I32 = jnp.int32

D_MODEL = 1024
M_HEADS = 4
M_HEAD_DIM = 128
M_WIDTH = M_HEADS * M_HEAD_DIM
CONV_WIDTH = 4
A_HEAD_DIM = 64
A_HEADS = 8
A_KV_HEADS = 4
A_GROUP = A_HEADS // A_KV_HEADS
A_WIDTH = A_HEADS * A_HEAD_DIM
A_KV_WIDTH = A_KV_HEADS * A_HEAD_DIM
IDX_HEADS = 8
IDX_DIM = 64
TOPK_MAX = 256
PAGE_SIZE = 128
ROPE_THETA = 10000.0
D_FF = 2816
NORM_EPS = 1e-6

LANES = 128
SUBLANES = 8
INT_MIN = -(2 ** 31)
NEG_BIG = -(2.0 ** 100)
F32_LOWEST = -3.0e38
LOG2E = math.log2(math.e)

PM_W = 4 * M_WIDTH
PA_W = A_WIDTH + 2 * A_KV_WIDTH
PI_W = IDX_HEADS * IDX_DIM + LANES
PG_W = LANES
MAIN_W = PM_W + PA_W + PI_W

ROW_TILE = 256
MLSTM_CHUNK = 256
DSA_TQ = 256
DSA_TK = 512
COUNT_ROWS = 128
ATT_HEADS_PER_LOOP = 4
SAMPLE_ROWS = SUBLANES
N_BISECT = 16
MAX_WALK = 8
VMEM_LIMIT = 56 * 1024 * 1024


def _cparams(sem):
    return pltpu.CompilerParams(dimension_semantics=sem, vmem_limit_bytes=VMEM_LIMIT)


def _split2(x):
    hi = x.astype(BF16)
    lo = (x - hi.astype(F32)).astype(BF16)
    return hi, lo


def _split3(x):
    a = x.astype(BF16)
    r = x - a.astype(F32)
    b = r.astype(BF16)
    c = (r - b.astype(F32)).astype(BF16)
    return a, b, c


def _dot(a, b):
    return jnp.dot(a, b, preferred_element_type=F32)


def _dot_nt(a, b):
    return lax.dot_general(a, b, (((1,), (1,)), ((), ())), preferred_element_type=F32)


def _proj_kernel(x_ref, g_ref, w_ref, wgh_ref, wgl_ref, pm_ref, pa_ref, pi_ref, pg_ref):
    x = x_ref[...]
    ms = jnp.mean(x * x, axis=-1, keepdims=True)
    y = x * lax.rsqrt(ms + NORM_EPS) * g_ref[...]
    yh, yl = _split2(y)
    pm_ref[...] = _dot(yh, w_ref[:, 0:PM_W])
    pa_ref[...] = _dot(yh, w_ref[:, PM_W:PM_W + PA_W])
    pi_ref[...] = _dot(yh, w_ref[:, PM_W + PA_W:MAIN_W])
    pg_ref[...] = _dot(yh, wgh_ref[...]) + _dot(yh, wgl_ref[...]) + _dot(yl, wgh_ref[...])


def _project(x, g, w_main, wg_hi, wg_lo):
    n = x.shape[0]
    tm = min(ROW_TILE, n)
    const = lambda i: (0, 0)
    return pl.pallas_call(
        _proj_kernel,
        grid=(n // tm,),
        in_specs=[
            pl.BlockSpec((tm, D_MODEL), lambda i: (i, 0)),
            pl.BlockSpec((1, D_MODEL), const),
            pl.BlockSpec((D_MODEL, MAIN_W), const),
            pl.BlockSpec((D_MODEL, PG_W), const),
            pl.BlockSpec((D_MODEL, PG_W), const),
        ],
        out_specs=[
            pl.BlockSpec((tm, PM_W), lambda i: (i, 0)),
            pl.BlockSpec((tm, PA_W), lambda i: (i, 0)),
            pl.BlockSpec((tm, PI_W), lambda i: (i, 0)),
            pl.BlockSpec((tm, PG_W), lambda i: (i, 0)),
        ],
        out_shape=[
            jax.ShapeDtypeStruct((n, PM_W), F32),
            jax.ShapeDtypeStruct((n, PA_W), F32),
            jax.ShapeDtypeStruct((n, PI_W), F32),
            jax.ShapeDtypeStruct((n, PG_W), F32),
        ],
        compiler_params=_cparams(("arbitrary",)),
        name="proj",
    )(x, g, w_main, wg_hi, wg_lo)


def _log_sigmoid(x):
    return jnp.minimum(x, 0.0) - jnp.log1p(jnp.exp(-jnp.abs(x)))


def _mlstm_kernel(*refs, lc, tb, tv, nc, has_state):
    if has_state:
        (pm_ref, pg_ref, cw_ref, bias_ref, gm_ref, c0_ref, n0_ref, m0_ref, conv0_ref,
         h_ref, c_out_ref, n_out_ref, m_out_ref, conv_out_ref,
         c_scr, n_scr, m_scr, ext_scr, u_scr) = refs
    else:
        (pm_ref, pg_ref, cw_ref, bias_ref, gm_ref,
         h_ref, c_out_ref, n_out_ref, m_out_ref, conv_out_ref,
         c_scr, n_scr, m_scr, ext_scr, u_scr) = refs
    c = pl.program_id(1)
    qk_w = 2 * M_WIDTH

    @pl.when(c == 0)
    def _init():
        ext_scr[0:SUBLANES, :] = jnp.zeros((SUBLANES, qk_w), F32)
        if has_state:
            c_scr[...] = c0_ref[0]
            n_scr[...] = n0_ref[0]
            m_scr[...] = m0_ref[0]
            ext_scr[SUBLANES - (CONV_WIDTH - 1):SUBLANES, :] = conv0_ref[0]
        else:
            c_scr[...] = jnp.zeros(c_scr.shape, F32)
            n_scr[...] = jnp.zeros(n_scr.shape, F32)
            m_scr[...] = jnp.zeros(m_scr.shape, F32)

    if tb < lc:
        u_scr[...] = jnp.zeros(u_scr.shape, F32)
        u_scr[0:tb, 0:PM_W] = pm_ref[0]
        u_scr[0:tb, PM_W:PM_W + PG_W] = pg_ref[0]
        pm = u_scr[:, 0:PM_W]
        gates = u_scr[:, PM_W:PM_W + PG_W]
    else:
        pm = pm_ref[0]
        gates = pg_ref[0]

    ext_scr[SUBLANES:SUBLANES + lc, :] = pm[:, 0:qk_w]
    cw = cw_ref[...]
    qk = ext_scr[pl.ds(SUBLANES - 3, lc), :] * cw[0:1, :]
    for j in range(1, CONV_WIDTH):
        qk = qk + ext_scr[pl.ds(SUBLANES - 3 + j, lc), :] * cw[j:j + 1, :]
    new_tail = ext_scr[pl.ds(tv, SUBLANES), :]
    qk = qk * jax.nn.sigmoid(qk)

    a = gates + bias_ref[...]
    ig = a
    lf = _log_sigmoid(a)
    if tv < lc:
        valid = lax.broadcasted_iota(I32, (lc, LANES), 0) < tv
        ig = jnp.where(valid, ig, NEG_BIG)
        lf = jnp.where(valid, lf, 0.0)
    ig_t = ig.T[0:SUBLANES, :]
    lf_t = lf.T[0:SUBLANES, :]

    r_i = lax.broadcasted_iota(I32, (lc, lc), 0)
    c_i = lax.broadcasted_iota(I32, (lc, lc), 1)
    causal = c_i <= r_i
    tril = jnp.where(causal, 1.0, 0.0).astype(BF16)
    triu = jnp.where(r_i <= c_i, 1.0, 0.0).astype(BF16)
    b_cols = sum(_dot(tril, p) for p in _split3(lf))
    b_rows = sum(_dot(p, triu) for p in _split3(lf_t))

    outs = []
    for h in range(M_HEADS):
        lo, hi = h * M_HEAD_DIM, (h + 1) * M_HEAD_DIM
        q = qk[:, lo:hi]
        k = qk[:, M_WIDTH + lo:M_WIDTH + hi] * (M_HEAD_DIM ** -0.5)
        v = pm[:, 2 * M_WIDTH + lo:2 * M_WIDTH + hi]
        og = pm[:, 3 * M_WIDTH + lo:3 * M_WIDTH + hi]
        b_col = b_cols[:, M_HEADS + h:M_HEADS + h + 1]
        i_col = ig[:, h:h + 1]
        b_row = b_rows[M_HEADS + h:M_HEADS + h + 1, :]
        i_row = ig_t[h:h + 1, :]
        m_prev = m_scr[h][:, 0:1]
        c_h = c_scr[h]
        n_h = n_scr[h]

        log_d = jnp.where(causal, b_col - b_row + i_row, NEG_BIG)
        inter = b_col + m_prev
        m_t = jnp.maximum(jnp.max(log_d, axis=-1, keepdims=True), inter)
        w_inter = jnp.exp(inter - m_t)
        d_mat = jnp.exp(log_d - m_t)
        qb = q.astype(BF16)
        s = _dot_nt(qb, k.astype(BF16)) * d_mat
        num = w_inter * _dot_nt(qb, c_h.astype(BF16)) + _dot(s.astype(BF16), v.astype(BF16))
        nq = w_inter * jnp.sum(q * n_h, axis=-1, keepdims=True) + jnp.sum(s, axis=-1, keepdims=True)
        hh = num / jnp.maximum(jnp.abs(nq), jnp.exp(-m_t))
        ms = jnp.mean(hh * hh, axis=-1, keepdims=True)
        hn = hh * lax.rsqrt(ms + NORM_EPS) * gm_ref[:, lo:hi]
        outs.append(jax.nn.sigmoid(og) * hn)

        b_last = b_cols[lc - 1:lc, M_HEADS + h:M_HEADS + h + 1]
        log_w = b_last - b_col + i_col
        m_new = jnp.maximum(b_last + m_prev, jnp.max(log_w, axis=0, keepdims=True))
        decay = jnp.exp(b_last + m_prev - m_new)
        kw = k * jnp.exp(log_w - m_new)
        c_scr[h] = decay * c_h + _dot(v.T.astype(BF16), kw.astype(BF16))
        n_scr[h] = decay * n_h + jnp.sum(kw, axis=0, keepdims=True)
        m_scr[h] = jnp.broadcast_to(m_new, (1, LANES))

    out = jnp.concatenate(outs, axis=-1)
    h_ref[0] = out[0:tb].astype(h_ref.dtype)
    ext_scr[0:SUBLANES, :] = new_tail

    @pl.when(c == nc - 1)
    def _fin():
        c_out_ref[0] = c_scr[...]
        n_out_ref[0] = n_scr[...]
        m_out_ref[0] = m_scr[...]
        conv_out_ref[0] = new_tail[SUBLANES - (CONV_WIDTH - 1):SUBLANES, :]


def _mlstm(pm, pg, conv_w, bias_row, g_mout, state):
    b, t, _ = pm.shape
    if t >= MLSTM_CHUNK:
        lc, tb, tv = MLSTM_CHUNK, MLSTM_CHUNK, MLSTM_CHUNK
    else:
        lc, tb, tv = max(SUBLANES, -(-t // SUBLANES) * SUBLANES), t, t
    nc = max(t // lc, 1)
    has_state = state is not None
    qk_w = 2 * M_WIDTH
    const2 = lambda i, j: (0, 0)
    per_b4 = lambda i, j: (i, 0, 0, 0)
    in_specs = [
        pl.BlockSpec((1, tb, PM_W), lambda i, j: (i, j, 0)),
        pl.BlockSpec((1, tb, PG_W), lambda i, j: (i, j, 0)),
        pl.BlockSpec((CONV_WIDTH, qk_w), const2),
        pl.BlockSpec((1, PG_W), const2),
        pl.BlockSpec((1, M_WIDTH), const2),
    ]
    args = [pm, pg, conv_w, bias_row, g_mout]
    if has_state:
        in_specs += [
            pl.BlockSpec((1, M_HEADS, M_HEAD_DIM, M_HEAD_DIM), per_b4),
            pl.BlockSpec((1, M_HEADS, 1, M_HEAD_DIM), per_b4),
            pl.BlockSpec((1, M_HEADS, 1, LANES), per_b4),
            pl.BlockSpec((1, CONV_WIDTH - 1, qk_w), lambda i, j: (i, 0, 0)),
        ]
        args += list(state)
    out_specs = [
        pl.BlockSpec((1, tb, M_WIDTH), lambda i, j: (i, j, 0)),
        pl.BlockSpec((1, M_HEADS, M_HEAD_DIM, M_HEAD_DIM), per_b4),
        pl.BlockSpec((1, M_HEADS, 1, M_HEAD_DIM), per_b4),
        pl.BlockSpec((1, M_HEADS, 1, LANES), per_b4),
        pl.BlockSpec((1, CONV_WIDTH - 1, qk_w), lambda i, j: (i, 0, 0)),
    ]
    out_shape = [
        jax.ShapeDtypeStruct((b, t, M_WIDTH), BF16),
        jax.ShapeDtypeStruct((b, M_HEADS, M_HEAD_DIM, M_HEAD_DIM), F32),
        jax.ShapeDtypeStruct((b, M_HEADS, 1, M_HEAD_DIM), F32),
        jax.ShapeDtypeStruct((b, M_HEADS, 1, LANES), F32),
        jax.ShapeDtypeStruct((b, CONV_WIDTH - 1, qk_w), F32),
    ]
    scratch = [
        pltpu.VMEM((M_HEADS, M_HEAD_DIM, M_HEAD_DIM), F32),
        pltpu.VMEM((M_HEADS, 1, M_HEAD_DIM), F32),
        pltpu.VMEM((M_HEADS, 1, LANES), F32),
        pltpu.VMEM((lc + 2 * SUBLANES, qk_w), F32),
        pltpu.VMEM((lc, PM_W + PG_W), F32),
    ]
    kern = functools.partial(_mlstm_kernel, lc=lc, tb=tb, tv=tv, nc=nc, has_state=has_state)
    return pl.pallas_call(
        kern, grid=(b, nc), in_specs=in_specs, out_specs=out_specs, out_shape=out_shape,
        scratch_shapes=scratch, compiler_params=_cparams(("arbitrary", "arbitrary")), name="mlstm",
    )(*args)


def _swap_halves(x):
    lane = lax.broadcasted_iota(I32, x.shape, 1)
    first = (lane % A_HEAD_DIM) < (A_HEAD_DIM // 2)
    return jnp.where(first, pltpu.roll(x, LANES - A_HEAD_DIM // 2, 1), pltpu.roll(x, A_HEAD_DIM // 2, 1))


def _rope(x, cos, sin):
    return x * cos + _swap_halves(x) * sin


def _head_rms(x, seg, g):
    hi, lo = _split2(x * x)
    ss = _dot(hi, seg) + _dot(lo, seg)
    return x * lax.rsqrt(ss * (1.0 / A_HEAD_DIM) + NORM_EPS) * g


def _prep_heads(pa_blk, pi_blk, cos, sin, gq, gk, seg,
                q_ref, kh_ref, vd_ref, qi_ref, kib_ref, k_out_ref, v_out_ref, ki_out_ref):
    lane = lax.broadcasted_iota(I32, cos.shape, 1)
    low = lane < A_HEAD_DIM
    half = A_HEAD_DIM

    def split_heads(blk):
        return blk[:, 0:half], pltpu.roll(blk, half, 1)[:, 0:half]

    for j in range(A_WIDTH // LANES):
        blk = _rope(_head_rms(pa_blk(j * LANES), seg, gq), cos, sin)
        blk = blk * (A_HEAD_DIM ** -0.5 * LOG2E)
        h0, h1 = split_heads(blk)
        q_ref[0, 2 * j] = h0.astype(BF16)
        q_ref[0, 2 * j + 1] = h1.astype(BF16)
    for j in range(A_KV_WIDTH // LANES):
        blk = _rope(_head_rms(pa_blk(A_WIDTH + j * LANES), seg, gk), cos, sin)
        k_out_ref[0, :, j * LANES:(j + 1) * LANES] = blk
        h0, h1 = split_heads(blk)
        kh_ref[0, 2 * j] = h0.astype(BF16)
        kh_ref[0, 2 * j + 1] = h1.astype(BF16)
        vb = pa_blk(A_WIDTH + A_KV_WIDTH + j * LANES)
        v_out_ref[0, :, j * LANES:(j + 1) * LANES] = vb
        vr = pltpu.roll(vb, half, 1)
        ones_col = jnp.where(lane == half, 1.0, 0.0)
        vd_ref[0, 2 * j] = jnp.where(low, vb, ones_col).astype(BF16)
        vd_ref[0, 2 * j + 1] = jnp.where(low, vr, ones_col).astype(BF16)
    for j in range(IDX_HEADS * IDX_DIM // LANES):
        blk = _rope(pi_blk(j * LANES), cos, sin) * (IDX_DIM ** -0.5)
        h0, h1 = split_heads(blk)
        qi_ref[0, 2 * j] = h0.astype(BF16)
        qi_ref[0, 2 * j + 1] = h1.astype(BF16)
    kblk = _rope(pi_blk(IDX_HEADS * IDX_DIM), cos, sin)[:, 0:half]
    ki_out_ref[0] = kblk
    kib_ref[0] = kblk.astype(BF16)


def _aprep_kernel(pa_ref, pi_ref, cos_ref, sin_ref, gq_ref, gk_ref, seg_ref, *out_refs):
    _prep_heads(lambda c0: pa_ref[0, :, c0:c0 + LANES], lambda c0: pi_ref[0, :, c0:c0 + LANES],
                cos_ref[...], sin_ref[...], gq_ref[...], gk_ref[...], seg_ref[...], *out_refs)


def _proj_prep_kernel(x_ref, g_ref, w_ref, wgh_ref, wgl_ref, cos_ref, sin_ref, gq_ref, gk_ref, seg_ref,
                      pm_ref, pg_ref, *head_refs):
    x = x_ref[...]
    ms = jnp.mean(x * x, axis=-1, keepdims=True)
    y = x * lax.rsqrt(ms + NORM_EPS) * g_ref[...]
    yh, yl = _split2(y)
    pm_ref[...] = _dot(yh, w_ref[:, 0:PM_W])
    pg_ref[...] = _dot(yh, wgh_ref[...]) + _dot(yh, wgl_ref[...]) + _dot(yl, wgh_ref[...])
    pa = _dot(yh, w_ref[:, PM_W:PM_W + PA_W])
    pi = _dot(yh, w_ref[:, PM_W + PA_W:MAIN_W])
    _prep_heads(lambda c0: pa[:, c0:c0 + LANES], lambda c0: pi[:, c0:c0 + LANES],
                cos_ref[...], sin_ref[...], gq_ref[...], gk_ref[...], seg_ref[...], *head_refs)


def _proj_prep(x, seq_len, g, w_main, wg_hi, wg_lo, cos, sin, gq, gk, seg):
    n = x.shape[0]
    b = n // seq_len
    tm = min(ROW_TILE, seq_len)
    nsb = seq_len // tm
    const = lambda i: (0, 0)
    rows = lambda i: (i, 0)
    pos = lambda i: (i % nsb, 0)
    hm = lambda i: (i // nsb, 0, i % nsb, 0)
    row3 = lambda i: (i // nsb, i % nsb, 0)
    return pl.pallas_call(
        _proj_prep_kernel,
        grid=(n // tm,),
        in_specs=[
            pl.BlockSpec((tm, D_MODEL), rows),
            pl.BlockSpec((1, D_MODEL), const),
            pl.BlockSpec((D_MODEL, MAIN_W), const),
            pl.BlockSpec((D_MODEL, PG_W), const),
            pl.BlockSpec((D_MODEL, PG_W), const),
            pl.BlockSpec((tm, LANES), pos),
            pl.BlockSpec((tm, LANES), pos),
            pl.BlockSpec((1, LANES), const),
            pl.BlockSpec((1, LANES), const),
            pl.BlockSpec((LANES, LANES), const),
        ],
        out_specs=[
            pl.BlockSpec((tm, PM_W), rows),
            pl.BlockSpec((tm, PG_W), rows),
            pl.BlockSpec((1, A_HEADS, tm, A_HEAD_DIM), hm),
            pl.BlockSpec((1, A_KV_HEADS, tm, A_HEAD_DIM), hm),
            pl.BlockSpec((1, A_KV_HEADS, tm, LANES), hm),
            pl.BlockSpec((1, IDX_HEADS, tm, IDX_DIM), hm),
            pl.BlockSpec((1, tm, IDX_DIM), row3),
            pl.BlockSpec((1, tm, A_KV_WIDTH), row3),
            pl.BlockSpec((1, tm, A_KV_WIDTH), row3),
            pl.BlockSpec((1, tm, IDX_DIM), row3),
        ],
        out_shape=[
            jax.ShapeDtypeStruct((n, PM_W), F32),
            jax.ShapeDtypeStruct((n, PG_W), F32),
            jax.ShapeDtypeStruct((b, A_HEADS, seq_len, A_HEAD_DIM), BF16),
            jax.ShapeDtypeStruct((b, A_KV_HEADS, seq_len, A_HEAD_DIM), BF16),
            jax.ShapeDtypeStruct((b, A_KV_HEADS, seq_len, LANES), BF16),
            jax.ShapeDtypeStruct((b, IDX_HEADS, seq_len, IDX_DIM), BF16),
            jax.ShapeDtypeStruct((b, seq_len, IDX_DIM), BF16),
            jax.ShapeDtypeStruct((b, seq_len, A_KV_WIDTH), F32),
            jax.ShapeDtypeStruct((b, seq_len, A_KV_WIDTH), F32),
            jax.ShapeDtypeStruct((b, seq_len, IDX_DIM), F32),
        ],
        compiler_params=_cparams(("arbitrary",)),
        name="proj_prep",
    )(x, g, w_main, wg_hi, wg_lo, cos, sin, gq, gk, seg)


def _aprep(pa, pi, cos, sin, gq, gk, seg):
    b, t, _ = pa.shape
    tm = min(ROW_TILE, t)
    c2 = lambda i, j: (0, 0)
    row3 = lambda i, j: (i, j, 0)
    hm = lambda i, j: (i, 0, j, 0)
    return pl.pallas_call(
        _aprep_kernel,
        grid=(b, t // tm),
        in_specs=[
            pl.BlockSpec((1, tm, PA_W), row3),
            pl.BlockSpec((1, tm, PI_W), row3),
            pl.BlockSpec((tm, LANES), lambda i, j: (j, 0)),
            pl.BlockSpec((tm, LANES), lambda i, j: (j, 0)),
            pl.BlockSpec((1, LANES), c2),
            pl.BlockSpec((1, LANES), c2),
            pl.BlockSpec((LANES, LANES), c2),
        ],
        out_specs=[
            pl.BlockSpec((1, A_HEADS, tm, A_HEAD_DIM), hm),
            pl.BlockSpec((1, A_KV_HEADS, tm, A_HEAD_DIM), hm),
            pl.BlockSpec((1, A_KV_HEADS, tm, LANES), hm),
            pl.BlockSpec((1, IDX_HEADS, tm, IDX_DIM), hm),
            pl.BlockSpec((1, tm, IDX_DIM), row3),
            pl.BlockSpec((1, tm, A_KV_WIDTH), row3),
            pl.BlockSpec((1, tm, A_KV_WIDTH), row3),
            pl.BlockSpec((1, tm, IDX_DIM), row3),
        ],
        out_shape=[
            jax.ShapeDtypeStruct((b, A_HEADS, t, A_HEAD_DIM), BF16),
            jax.ShapeDtypeStruct((b, A_KV_HEADS, t, A_HEAD_DIM), BF16),
            jax.ShapeDtypeStruct((b, A_KV_HEADS, t, LANES), BF16),
            jax.ShapeDtypeStruct((b, IDX_HEADS, t, IDX_DIM), BF16),
            jax.ShapeDtypeStruct((b, t, IDX_DIM), BF16),
            jax.ShapeDtypeStruct((b, t, A_KV_WIDTH), F32),
            jax.ShapeDtypeStruct((b, t, A_KV_WIDTH), F32),
            jax.ShapeDtypeStruct((b, t, IDX_DIM), F32),
        ],
        compiler_params=_cparams(("arbitrary", "arbitrary")),
        name="aprep",
    )(pa, pi, cos, sin, gq, gk, seg)


def _sortable_key(score):
    bits = lax.bitcast_convert_type(score, I32)
    key = bits ^ ((bits >> 31) & 0x7FFFFFFF)
    key = jnp.where(key == -1, 0, key)
    return jnp.where(score == -jnp.inf, INT_MIN, key)


def _fold_lanes(x):
    acc = x[:, 0:LANES]
    for j in range(1, x.shape[1] // LANES):
        acc = acc + x[:, j * LANES:(j + 1) * LANES]
    return acc


def _fold_lanes_min(x):
    acc = x[:, 0:LANES]
    for j in range(1, x.shape[1] // LANES):
        acc = jnp.minimum(acc, x[:, j * LANES:(j + 1) * LANES])
    return acc


def _fold_lanes_max(x):
    acc = x[:, 0:LANES]
    for j in range(1, x.shape[1] // LANES):
        acc = jnp.maximum(acc, x[:, j * LANES:(j + 1) * LANES])
    return acc


def _row_count(mask):
    return jnp.sum(_fold_lanes(jnp.where(mask, 1.0, 0.0)), axis=-1, keepdims=True)


def _bisect(count_ge, lo, hi, k_top):
    def body(_, c):
        lo, hi = c
        mid = lo + (hi - lo) * 0.5
        ge = count_ge(mid) >= k_top
        return jnp.where(ge, mid, lo), jnp.where(ge, hi, mid)
    return lax.fori_loop(0, N_BISECT, body, (lo, hi))[0]


def _selected(sc, thr, strict):
    at_least = jnp.where(strict > 0.0, F32_LOWEST, thr)
    above = jnp.where(strict > 0.0, thr, F32_LOWEST)
    return jnp.logical_and(sc >= at_least, sc > above)


def _walk_up(lo, c_lo, k_top, min_selected, count_gt):
    def cond(c):
        return jnp.logical_and(c[5], c[6] < MAX_WALK)

    def body(c):
        thr, strict, c_sel, c_gt, tie, _, it = c
        active = jnp.logical_and(c_sel > k_top, tie == 0.0)
        v = min_selected(thr, strict)
        c_above = count_gt(v)
        adv = jnp.logical_and(active, c_above >= k_top)
        stop = jnp.logical_and(active, c_above < k_top)
        again = jnp.max(jnp.where(jnp.logical_and(adv, c_above > k_top), 1.0, 0.0)) > 0.0
        return (jnp.where(active, v, thr), jnp.where(adv, 1.0, jnp.where(stop, 0.0, strict)),
                jnp.where(adv, c_above, c_sel), jnp.where(stop, c_above, c_gt), jnp.where(stop, 1.0, tie),
                again, it + 1)

    zero = jnp.zeros_like(lo)
    thr, strict, _, c_gt, tie, pending, _ = lax.while_loop(
        cond, body, (lo, zero, c_lo, zero, zero, jnp.max(c_lo) > k_top, jnp.int32(0)))
    return thr, strict, c_gt, tie, pending


def _kth_key(count_ge_key, rows, k_top):
    def body(it, thr):
        cand = thr ^ lax.shift_left(jnp.int32(1), 31 - it)
        return jnp.where(count_ge_key(cand) >= k_top, cand, thr)
    return lax.fori_loop(0, 32, body, jnp.full((rows, 1), INT_MIN, I32))


def _tie_rank(eq, before):
    n = eq.shape[1]
    r_i = lax.broadcasted_iota(I32, (n, n), 0)
    c_i = lax.broadcasted_iota(I32, (n, n), 1)
    triu = jnp.where(r_i <= c_i, 1.0, 0.0).astype(BF16)
    return before + _dot(jnp.where(eq, 1.0, 0.0).astype(BF16), triu)


def _dense_tie_ranks(eq):
    before = jnp.zeros((eq.shape[0], 1), F32)
    ranks = []
    for j in range(eq.shape[1] // LANES):
        eq_j = eq[:, j * LANES:(j + 1) * LANES]
        ranks.append(_tie_rank(eq_j, before))
        before = before + _row_count(eq_j)
    return jnp.concatenate(ranks, axis=-1)


def _select_dense(sc, k_top, bias_ref):
    rows, n = sc.shape
    lo0 = jnp.min(jnp.where(sc == -jnp.inf, jnp.inf, sc), axis=-1, keepdims=True)
    hi0 = jnp.max(sc, axis=-1, keepdims=True)
    count_ge = lambda thr: _row_count(sc >= thr)
    lo = _bisect(count_ge, lo0, hi0, k_top)
    c_lo = count_ge(lo)
    bias_ref[...] = jnp.where(sc >= lo, 0.0, NEG_BIG)

    @pl.when(jnp.max(c_lo) > k_top)
    def _refine():
        min_selected = lambda thr, strict: jnp.min(
            _fold_lanes_min(jnp.where(_selected(sc, thr, strict), sc, jnp.inf)), axis=-1, keepdims=True)
        thr, strict, c_gt, tie, unresolved = _walk_up(lo, c_lo, k_top, min_selected, lambda v: _row_count(sc > v))

        @pl.when(jnp.logical_not(unresolved))
        def _ties():
            eq = jnp.logical_and(sc == thr, tie > 0.0)
            drop = jnp.logical_and(eq, _dense_tie_ranks(eq) > k_top - c_gt)
            keep = jnp.logical_and(_selected(sc, thr, strict), jnp.logical_not(drop))
            bias_ref[...] = jnp.where(keep, 0.0, NEG_BIG)

        @pl.when(unresolved)
        def _exact():
            key = _sortable_key(sc)
            tk_ = _kth_key(lambda c: _row_count(key >= c), rows, k_top)
            eq = key == tk_
            need = k_top - (_row_count(key >= tk_) - _row_count(eq))
            keep_eq = jnp.logical_and(eq, jnp.logical_and(_dense_tie_ranks(eq) <= need, tk_ > INT_MIN))
            bias_ref[...] = jnp.where(jnp.logical_or(key > tk_, keep_eq), 0.0, NEG_BIG)


def _dsa_prompt_kernel(q_ref, k_ref, v_ref, qi_ref, ki_ref, pg_ref, o_ref, sc_ref, thr_ref, strict_ref,
                       *, tq, tk, k_top):
    i = pl.program_id(1)
    r0 = i * tq
    nkb = (r0 + tq + tk - 1) // tk
    row = r0 + lax.broadcasted_iota(I32, (tq, tk), 0)
    col0 = lax.broadcasted_iota(I32, (tq, tk), 1)

    qi = qi_ref[0].reshape(IDX_HEADS * tq, IDX_DIM)
    w = pg_ref[0][:, 2 * M_HEADS:2 * M_HEADS + IDX_HEADS] * (IDX_HEADS ** -0.5)

    def score_body(kb, carry):
        lo, hi = carry
        start = pl.multiple_of(kb * tk, tk)
        d = _dot_nt(qi, ki_ref[0, pl.ds(start, tk), :])
        d = jnp.maximum(d, 0.0).reshape(IDX_HEADS, tq, tk)
        sc = d[0] * w[:, 0:1]
        for h in range(1, IDX_HEADS):
            sc = sc + d[h] * w[:, h:h + 1]
        vis = col0 + kb * tk <= row
        sc_vis = jnp.where(vis, sc, -jnp.inf)
        sc_ref[kb] = sc_vis
        lo = jnp.minimum(lo, _fold_lanes_min(jnp.where(vis, sc, jnp.inf)))
        hi = jnp.maximum(hi, _fold_lanes_max(sc_vis))
        return lo, hi
    bounds = (jnp.full((tq, LANES), jnp.inf, F32), jnp.full((tq, LANES), -jnp.inf, F32))
    bounds = lax.fori_loop(0, nkb // 2, lambda i, c: score_body(2 * i + 1, score_body(2 * i, c)), bounds)
    lo0, hi0 = lax.cond(nkb % 2 == 1, lambda c: score_body(nkb - 1, c), lambda c: c, bounds)
    lo0 = jnp.min(lo0, axis=-1, keepdims=True)
    hi0 = jnp.max(hi0, axis=-1, keepdims=True)

    def count_blocks(pred):
        def body(kb, acc):
            return acc + _fold_lanes(jnp.where(pred(sc_ref[kb]), 1.0, 0.0))
        acc = lax.fori_loop(0, nkb, body, jnp.zeros((tq, LANES), F32))
        return jnp.sum(acc, axis=-1, keepdims=True)

    rep = lambda col: jnp.broadcast_to(col, (tq, LANES))

    def fold_tiles(tile_fn, init, combine, *reps):
        parts = []
        for r in range(0, tq, COUNT_ROWS):
            ops = [x[r:r + COUNT_ROWS] for x in reps]

            def body(kb, acc, r=r, ops=ops):
                for j in range(tk // LANES):
                    tile = sc_ref[kb, r:r + COUNT_ROWS, j * LANES:(j + 1) * LANES]
                    acc = combine(acc, tile_fn(tile, *ops))
                return acc
            parts.append(lax.fori_loop(0, nkb, body, jnp.full((COUNT_ROWS, LANES), init, F32)))
        return jnp.concatenate(parts, axis=0)

    ones_sq = jnp.ones((LANES, LANES), BF16)
    lane_sum = lambda part: _dot(part.astype(BF16), ones_sq)
    count_ge_rep = lambda t_rep: lane_sum(fold_tiles(lambda tile, t: jnp.where(tile >= t, 1.0, 0.0), 0.0, jnp.add, t_rep))

    def count_gt(v):
        return lane_sum(fold_tiles(lambda tile, t: jnp.where(tile > t, 1.0, 0.0), 0.0, jnp.add, rep(v)))[:, 0:1]

    def min_selected(thr, strict):
        at_least = rep(jnp.where(strict > 0.0, F32_LOWEST, thr))
        above = rep(jnp.where(strict > 0.0, thr, F32_LOWEST))
        pick = lambda tile, a, b: jnp.where(jnp.logical_and(tile >= a, tile > b), tile, jnp.inf)
        return jnp.min(fold_tiles(pick, jnp.inf, jnp.minimum, at_least, above), axis=-1, keepdims=True)

    lo = _bisect(count_ge_rep, rep(lo0), rep(hi0), k_top)[:, 0:1]
    c_lo = count_ge_rep(rep(lo))[:, 0:1]
    thr_ref[...] = rep(lo)
    strict_ref[...] = jnp.zeros((tq, LANES), F32)

    @pl.when(jnp.max(c_lo) > k_top)
    def _refine():
        thr, strict, c_gt, tie, unresolved = _walk_up(lo, c_lo, k_top, min_selected, count_gt)
        thr_ref[...] = jnp.broadcast_to(thr, (tq, LANES))
        strict_ref[...] = jnp.broadcast_to(strict, (tq, LANES))

        @pl.when(jnp.logical_and(jnp.logical_not(unresolved), jnp.max(tie) > 0.0))
        def _ties():
            def body(kb, before):
                sc = sc_ref[kb]
                eq = jnp.logical_and(sc == thr, tie > 0.0)
                drop = jnp.logical_and(eq, _tie_rank(eq, before) > k_top - c_gt)
                sc_ref[kb] = jnp.where(drop, -jnp.inf, sc)
                return before + _row_count(eq)
            lax.fori_loop(0, nkb, body, jnp.zeros((tq, 1), F32))

        @pl.when(unresolved)
        def _exact():
            tk_ = _kth_key(lambda c: count_blocks(lambda sc: _sortable_key(sc) >= c), tq, k_top)
            n_ge = count_blocks(lambda sc: _sortable_key(sc) >= tk_)
            n_eq = count_blocks(lambda sc: _sortable_key(sc) == tk_)
            need = k_top - (n_ge - n_eq)

            def body(kb, before):
                sc = sc_ref[kb]
                key = _sortable_key(sc)
                eq = key == tk_
                keep_eq = jnp.logical_and(eq, jnp.logical_and(_tie_rank(eq, before) <= need, tk_ > INT_MIN))
                sc_ref[kb] = jnp.where(jnp.logical_or(key > tk_, keep_eq), sc, -jnp.inf)
                return before + _row_count(eq)
            lax.fori_loop(0, nkb, body, jnp.zeros((tq, 1), F32))
            thr_ref[...] = jnp.full((tq, LANES), F32_LOWEST, F32)
            strict_ref[...] = jnp.zeros((tq, LANES), F32)

    thr = thr_ref[...][:, 0:1]
    strict = strict_ref[...][:, 0:1]
    rows2 = A_GROUP * tq

    lane = lax.broadcasted_iota(I32, (tq, LANES), 1)
    for g0 in range(0, A_KV_HEADS, ATT_HEADS_PER_LOOP):
        heads = range(g0, g0 + ATT_HEADS_PER_LOOP)

        def att_body(kb, carry, heads=heads):
            start = pl.multiple_of(kb * tk, tk)
            bias = jnp.where(_selected(sc_ref[kb], thr, strict), 0.0, NEG_BIG).astype(BF16)
            bias = jnp.concatenate([bias] * A_GROUP, axis=0)
            new = []
            for g, (m, acc) in zip(heads, carry):
                q2 = q_ref[0, A_GROUP * g:A_GROUP * (g + 1)].reshape(rows2, A_HEAD_DIM)
                s = _dot_nt(q2, k_ref[0, g, pl.ds(start, tk), :]).astype(BF16) + bias
                m_new = jnp.maximum(m, jnp.max(s, axis=-1, keepdims=True).astype(F32))
                p = jnp.exp2(s - m_new.astype(BF16))
                acc = jnp.exp2(m - m_new) * acc + _dot(p, v_ref[0, g, pl.ds(start, tk), :])
                new.append((m_new, acc))
            return tuple(new)
        init = tuple((jnp.full((rows2, 1), NEG_BIG, F32), jnp.zeros((rows2, LANES), F32)) for _ in heads)
        def run(first, count, c):
            for u in range(count):
                c = att_body(first + u, c)
            return c
        state = lax.fori_loop(0, nkb // 4, lambda i, c: run(4 * i, 4, c), init)
        done = 4 * (nkb // 4)
        state = lax.cond(nkb % 4 >= 2, lambda c: run(done, 2, c), lambda c: c, state)
        final = lax.cond(nkb % 2 == 1, lambda c: run(nkb - 1, 1, c), lambda c: c, state)
        for g, (_, acc) in zip(heads, final):
            o = acc / acc[:, A_HEAD_DIM:A_HEAD_DIM + 1]
            slab = jnp.where(lane < A_HEAD_DIM, o[0:tq], pltpu.roll(o[tq:2 * tq], A_HEAD_DIM, 1))
            o_ref[0, :, g * LANES:(g + 1) * LANES] = slab.astype(o_ref.dtype)


def _dsa_prompt(q_hm, k_hm, v_dup, qi_hm, ki_bf, pg):
    b, _, s, _ = q_hm.shape
    tq = min(DSA_TQ, s)
    tk = min(DSA_TK, s)
    k_top = min(TOPK_MAX, s // 4)
    qb = lambda i, j: (i, 0, j, 0)
    whole = lambda i, j: (i, 0, 0, 0)
    kern = functools.partial(_dsa_prompt_kernel, tq=tq, tk=tk, k_top=k_top)
    return pl.pallas_call(
        kern,
        grid=(b, s // tq),
        in_specs=[
            pl.BlockSpec((1, A_HEADS, tq, A_HEAD_DIM), qb),
            pl.BlockSpec((1, A_KV_HEADS, s, A_HEAD_DIM), whole),
            pl.BlockSpec((1, A_KV_HEADS, s, LANES), whole),
            pl.BlockSpec((1, IDX_HEADS, tq, IDX_DIM), qb),
            pl.BlockSpec((1, s, IDX_DIM), lambda i, j: (i, 0, 0)),
            pl.BlockSpec((1, tq, PG_W), lambda i, j: (i, j, 0)),
        ],
        out_specs=pl.BlockSpec((1, tq, A_WIDTH), lambda i, j: (i, j, 0)),
        out_shape=jax.ShapeDtypeStruct((b, s, A_WIDTH), BF16),
        scratch_shapes=[pltpu.VMEM((s // tk, tq, tk), F32), pltpu.VMEM((tq, LANES), F32),
                        pltpu.VMEM((tq, LANES), F32)],
        compiler_params=_cparams(("arbitrary", "arbitrary")),
        name="dsa_prompt",
    )(q_hm, k_hm, v_dup, qi_hm, ki_bf, pg)


def _page_copies(pt_ref, cidx_hbm, ck_hbm, cv_hbm, kidx_buf, k_buf, v_buf, sems, seq, slot, *, layer, n_pages):
    copies = []
    for p in range(n_pages):
        page = pt_ref[seq * n_pages + p]
        cols = pl.ds(p * PAGE_SIZE, PAGE_SIZE)
        copies.append(pltpu.make_async_copy(cidx_hbm.at[layer, page], kidx_buf.at[slot, :, cols], sems.at[slot, 0]))
        copies.append(pltpu.make_async_copy(ck_hbm.at[layer, page], k_buf.at[slot, :, :, cols], sems.at[slot, 1]))
        copies.append(pltpu.make_async_copy(cv_hbm.at[layer, page], v_buf.at[slot, :, :, cols], sems.at[slot, 2]))
    return copies


def _dsa_sample_kernel(pt_ref, q_ref, qi_ref, w_ref, knt_ref, vnt_ref, kint_ref, cidx_hbm, ck_hbm, cv_hbm,
                       o_ref, kidx_buf, k_buf, v_buf, bias_ref, sems, *, layer, n_pages, t_new, k_top):
    b = pl.program_id(0)
    slot = b % 2
    rows = SAMPLE_ROWS
    copies = functools.partial(_page_copies, pt_ref, cidx_hbm, ck_hbm, cv_hbm, kidx_buf, k_buf, v_buf, sems,
                               layer=layer, n_pages=n_pages)

    @pl.when(b == 0)
    def _first():
        for c in copies(0, 0):
            c.start()

    @pl.when(b + 1 < pl.num_programs(0))
    def _next():
        for c in copies(b + 1, 1 - slot):
            c.start()

    for c in copies(b, slot):
        c.wait()

    w = w_ref[0][:, 2 * M_HEADS:2 * M_HEADS + IDX_HEADS] * (IDX_HEADS ** -0.5)

    def scores(ki_t):
        d = jnp.maximum(_dot(qi_ref[0], ki_t), 0.0).reshape(IDX_HEADS, rows, ki_t.shape[1])
        sc = d[0] * w[:, 0:1]
        for h in range(1, IDX_HEADS):
            sc = sc + d[h] * w[:, h:h + 1]
        return sc

    t_i = lax.broadcasted_iota(I32, (rows, PAGE_SIZE), 0)
    j_i = lax.broadcasted_iota(I32, (rows, PAGE_SIZE), 1)
    vis = jnp.logical_and(j_i <= t_i, j_i < t_new)
    sc = jnp.concatenate([scores(kidx_buf[slot].astype(BF16)),
                          jnp.where(vis, scores(kint_ref[0]), -jnp.inf)], axis=-1)
    _select_dense(sc, k_top, bias_ref)
    bias = jnp.concatenate([bias_ref[...]] * A_GROUP, axis=0)
    for h in range(A_KV_HEADS):
        qh = q_ref[0, h]
        s = jnp.concatenate([_dot(qh, k_buf[slot, h].astype(BF16)), _dot(qh, knt_ref[0, h])], axis=-1) + bias
        pr = jnp.exp2(s - jnp.max(s, axis=-1, keepdims=True))
        prb = pr.astype(BF16)
        past = n_pages * PAGE_SIZE
        o = _dot_nt(prb[:, 0:past], v_buf[slot, h].astype(BF16)) + _dot_nt(prb[:, past:], vnt_ref[0, h])
        o = o / jnp.sum(pr, axis=-1, keepdims=True)
        o_ref[0, A_GROUP * h:A_GROUP * (h + 1)] = o.reshape(A_GROUP, rows, A_HEAD_DIM).astype(o_ref.dtype)


def _dsa_sample(q_g, qi_all, w8, kn_t, vn_t, kin_t, ck_t, cv_t, cidx_t, page_table, *, layer, t_new):
    db = q_g.shape[0]
    n_pages = page_table.shape[1]
    past = n_pages * PAGE_SIZE
    k_top = min(TOPK_MAX, (past + t_new) // 4)
    per_b3 = lambda b, pt: (b, 0, 0)
    per_b4 = lambda b, pt: (b, 0, 0, 0)
    hbm = pl.BlockSpec(memory_space=pl.ANY)
    grid_spec = pltpu.PrefetchScalarGridSpec(
        num_scalar_prefetch=1,
        grid=(db,),
        in_specs=[
            pl.BlockSpec((1, A_KV_HEADS, A_GROUP * SAMPLE_ROWS, A_HEAD_DIM), per_b4),
            pl.BlockSpec((1, IDX_HEADS * SAMPLE_ROWS, IDX_DIM), per_b3),
            pl.BlockSpec((1, SAMPLE_ROWS, PG_W), per_b3),
            pl.BlockSpec((1, A_KV_HEADS, A_HEAD_DIM, PAGE_SIZE), per_b4),
            pl.BlockSpec((1, A_KV_HEADS, A_HEAD_DIM, PAGE_SIZE), per_b4),
            pl.BlockSpec((1, IDX_DIM, PAGE_SIZE), per_b3),
            hbm, hbm, hbm,
        ],
        out_specs=pl.BlockSpec((1, A_HEADS, SAMPLE_ROWS, A_HEAD_DIM), per_b4),
        scratch_shapes=[
            pltpu.VMEM((2, IDX_DIM, past), F32),
            pltpu.VMEM((2, A_KV_HEADS, A_HEAD_DIM, past), F32),
            pltpu.VMEM((2, A_KV_HEADS, A_HEAD_DIM, past), F32),
            pltpu.VMEM((SAMPLE_ROWS, past + PAGE_SIZE), F32),
            pltpu.SemaphoreType.DMA((2, 3)),
        ],
    )
    kern = functools.partial(_dsa_sample_kernel, layer=layer, n_pages=n_pages, t_new=t_new, k_top=k_top)
    return pl.pallas_call(
        kern, grid_spec=grid_spec,
        out_shape=jax.ShapeDtypeStruct((db, A_HEADS, SAMPLE_ROWS, A_HEAD_DIM), BF16),
        compiler_params=_cparams(("arbitrary",)),
        name="dsa_sample",
    )(page_table.reshape(-1), q_g, qi_all, w8, kn_t, vn_t, kin_t, cidx_t, ck_t, cv_t)


def _sample_layouts(q_hm, qi_hm, k_o, v_o, ki_o, pg):
    db, _, t, _ = q_hm.shape
    pad_t = SAMPLE_ROWS - t
    pad_rows = lambda x: jnp.pad(x, ((0, 0), (0, 0), (0, pad_t), (0, 0)))
    q_g = pad_rows(q_hm).reshape(db, A_KV_HEADS, A_GROUP * SAMPLE_ROWS, A_HEAD_DIM)
    qi_all = pad_rows(qi_hm).reshape(db, IDX_HEADS * SAMPLE_ROWS, IDX_DIM)
    w8 = jnp.pad(pg, ((0, 0), (0, pad_t), (0, 0)))

    def heads_t(x):
        xt = x.reshape(db, t, A_KV_HEADS, A_HEAD_DIM).transpose(0, 2, 3, 1)
        return jnp.pad(xt, ((0, 0), (0, 0), (0, 0), (0, PAGE_SIZE - t))).astype(BF16)

    kin_t = jnp.pad(ki_o.transpose(0, 2, 1), ((0, 0), (0, 0), (0, PAGE_SIZE - t))).astype(BF16)
    return q_g, qi_all, w8, heads_t(k_o), heads_t(v_o), kin_t


def _finish_kernel(x_ref, hm_ref, a_ref, wo_ref, gf_ref, wgu_ref, wd_ref, y_ref):
    x1 = (x_ref[...] + _dot(hm_ref[...], wo_ref[0:M_WIDTH, :]) + _dot(a_ref[...], wo_ref[M_WIDTH:M_WIDTH + A_WIDTH, :]))
    ms = jnp.mean(x1 * x1, axis=-1, keepdims=True)
    xn = (x1 * lax.rsqrt(ms + NORM_EPS) * gf_ref[...]).astype(BF16)
    g = _dot(xn, wgu_ref[:, 0:D_FF])
    u = _dot(xn, wgu_ref[:, D_FF:2 * D_FF])
    y_ref[...] = x1 + _dot((g * jax.nn.sigmoid(g) * u).astype(BF16), wd_ref[...])


def _finish(x, hm, a, w_out, g_ffn, w_gu, w_down):
    n = x.shape[0]
    tm = min(ROW_TILE, n)
    const = lambda i: (0, 0)
    row = lambda i: (i, 0)
    return pl.pallas_call(
        _finish_kernel,
        grid=(n // tm,),
        in_specs=[
            pl.BlockSpec((tm, D_MODEL), row),
            pl.BlockSpec((tm, M_WIDTH), row),
            pl.BlockSpec((tm, A_WIDTH), row),
            pl.BlockSpec((M_WIDTH + A_WIDTH, D_MODEL), const),
            pl.BlockSpec((1, D_MODEL), const),
            pl.BlockSpec((D_MODEL, 2 * D_FF), const),
            pl.BlockSpec((D_FF, D_MODEL), const),
        ],
        out_specs=pl.BlockSpec((tm, D_MODEL), row),
        out_shape=jax.ShapeDtypeStruct((n, D_MODEL), F32),
        compiler_params=_cparams(("arbitrary",)),
        name="finish",
    )(x, hm, a, w_out, g_ffn, w_gu, w_down)


def _rope_tables(pos):
    half = A_HEAD_DIM // 2
    inv = ROPE_THETA ** (-jnp.arange(half, dtype=F32) / half)
    ang = pos.astype(F32)[:, None] * inv[None, :]
    cos, sin = jnp.cos(ang), jnp.sin(ang)
    cos_t = jnp.tile(jnp.concatenate([cos, cos], axis=-1), (1, LANES // A_HEAD_DIM))
    sin_t = jnp.tile(jnp.concatenate([-sin, sin], axis=-1), (1, LANES // A_HEAD_DIM))
    return cos_t, sin_t


def _layer_weights(l, g_mix, w_in, conv_w, b_gate, g_mout, g_q, g_k, w_out, g_ffn, w_gate_up, w_down):
    w = w_in[l]
    o_mi = PM_W
    o_aq = o_mi + 2 * M_HEADS
    o_iw = o_aq + PA_W + IDX_HEADS * IDX_DIM + IDX_DIM
    w_main = jnp.concatenate(
        [w[:, 0:PM_W], w[:, o_aq:o_iw], jnp.zeros((D_MODEL, LANES - IDX_DIM), F32)], axis=1).astype(BF16)
    w_gate = jnp.concatenate(
        [w[:, o_mi:o_aq], w[:, o_iw:o_iw + IDX_HEADS], jnp.zeros((D_MODEL, PG_W - 2 * M_HEADS - IDX_HEADS), F32)], axis=1)
    wg_hi, wg_lo = _split2(w_gate)
    bias_row = jnp.concatenate([b_gate[l], jnp.zeros((PG_W - 2 * M_HEADS,), F32)])[None, :]
    tile2 = lambda g: jnp.tile(g, LANES // A_HEAD_DIM)[None, :]
    return dict(
        g_mix=g_mix[l][None, :], w_main=w_main, wg_hi=wg_hi, wg_lo=wg_lo,
        conv_w=conv_w[l], bias_row=bias_row, g_mout=g_mout[l][None, :],
        gq=tile2(g_q[l]), gk=tile2(g_k[l]),
        w_out=w_out[l].astype(BF16), g_ffn=g_ffn[l][None, :],
        w_gu=w_gate_up[l].astype(BF16), w_down=w_down[l].astype(BF16))


def kernel(x_prompt, x_sample, cache_k, cache_v, cache_kidx, page_table, state_C, state_n, state_m, state_conv,
           g_mix, w_in, conv_w, b_gate, g_mout, g_q, g_k, w_out, g_ffn, w_gate_up, w_down):
    b, s, _ = x_prompt.shape
    db, t, _ = x_sample.shape
    depth = w_in.shape[0]
    past = page_table.shape[1] * PAGE_SIZE
    cos_p, sin_p = _rope_tables(jnp.arange(s, dtype=I32))
    cos_s, sin_s = _rope_tables(past + jnp.arange(t, dtype=I32))
    seg = (jnp.arange(LANES)[:, None] // A_HEAD_DIM == jnp.arange(LANES)[None, :] // A_HEAD_DIM).astype(BF16)
    ck_t = cache_k.transpose(0, 1, 3, 4, 2)
    cv_t = cache_v.transpose(0, 1, 3, 4, 2)
    cidx_t = cache_kidx.transpose(0, 1, 3, 2)
    xp = x_prompt.reshape(b * s, D_MODEL)
    xs = x_sample.reshape(db * t, D_MODEL)
    kp, vp, kip, cp, np_, mp, bp = [], [], [], [], [], [], []
    ks_, vs_, kis, cs, ns, ms, bs = [], [], [], [], [], [], []
    for l in range(depth):
        wl = _layer_weights(l, g_mix, w_in, conv_w, b_gate, g_mout, g_q, g_k, w_out, g_ffn, w_gate_up, w_down)
        pm, pg, q_hm, k_hm, v_ext, qi_hm, ki_bf, k_o, v_o, ki_o = _proj_prep(
            xp, s, wl["g_mix"], wl["w_main"], wl["wg_hi"], wl["wg_lo"], cos_p, sin_p, wl["gq"], wl["gk"], seg)
        pm3, pg3 = pm.reshape(b, s, PM_W), pg.reshape(b, s, PG_W)
        hm, c_o, n_o, m_o, conv_o = _mlstm(pm3, pg3, wl["conv_w"], wl["bias_row"], wl["g_mout"], None)
        a = _dsa_prompt(q_hm, k_hm, v_ext, qi_hm, ki_bf, pg3)
        xp = _finish(xp, hm.reshape(b * s, M_WIDTH), a.reshape(b * s, A_WIDTH),
                     wl["w_out"], wl["g_ffn"], wl["w_gu"], wl["w_down"])
        kp.append(k_o.reshape(b, s, A_KV_HEADS, A_HEAD_DIM))
        vp.append(v_o.reshape(b, s, A_KV_HEADS, A_HEAD_DIM))
        kip.append(ki_o)
        cp.append(c_o)
        np_.append(n_o[:, :, 0, :])
        mp.append(m_o[:, :, 0, 0])
        bp.append(conv_o)

        pm, pa, pi, pg = _project(xs, wl["g_mix"], wl["w_main"], wl["wg_hi"], wl["wg_lo"])
        pg3 = pg.reshape(db, t, PG_W)
        state = (state_C[l], state_n[l][:, :, None, :],
                 jnp.broadcast_to(state_m[l][:, :, None, None], (db, M_HEADS, 1, LANES)), state_conv[l])
        hm, c_o, n_o, m_o, conv_o = _mlstm(pm.reshape(db, t, PM_W), pg3, wl["conv_w"], wl["bias_row"],
                                           wl["g_mout"], state)
        q_hm, _, _, qi_hm, _, k_o, v_o, ki_o = _aprep(
            pa.reshape(db, t, PA_W), pi.reshape(db, t, PI_W), cos_s, sin_s, wl["gq"], wl["gk"], seg)
        a = _dsa_sample(*_sample_layouts(q_hm, qi_hm, k_o, v_o, ki_o, pg3),
                        ck_t, cv_t, cidx_t, page_table, layer=l, t_new=t)
        a = a[:, :, 0:t, :].transpose(0, 2, 1, 3).reshape(db * t, A_WIDTH)
        xs = _finish(xs, hm.reshape(db * t, M_WIDTH), a, wl["w_out"], wl["g_ffn"], wl["w_gu"], wl["w_down"])
        ks_.append(k_o.reshape(db, t, A_KV_HEADS, A_HEAD_DIM))
        vs_.append(v_o.reshape(db, t, A_KV_HEADS, A_HEAD_DIM))
        kis.append(ki_o)
        cs.append(c_o)
        ns.append(n_o[:, :, 0, :])
        ms.append(m_o[:, :, 0, 0])
        bs.append(conv_o)
    return (xp.reshape(b, s, D_MODEL), xs.reshape(db, t, D_MODEL),
            jnp.stack(kp), jnp.stack(vp), jnp.stack(kip), jnp.stack(cp), jnp.stack(np_), jnp.stack(mp), jnp.stack(bp),
            jnp.stack(ks_), jnp.stack(vs_), jnp.stack(kis), jnp.stack(cs), jnp.stack(ns), jnp.stack(ms), jnp.stack(bs))
```
